```python
import math
import jax, jax.numpy as jnp
from jax import lax
import numpy as np

D_MODEL = 1024
BATCH = 4
SEQ = 4096
DEPTH = 4
DEC_BATCH = 32
DEC_SEQ = 4
PAST_LEN = 8192
PAGE_SIZE = 128

N_MIXERS = 3
N_META = 16
EPS = 1e-6
Q_BLOCK = 128
DSA_HEADS = 8
DSA_KV_HEADS = 2
HEAD_DIM = D_MODEL // DSA_HEADS
IDX_HEADS = 8
IDX_DIM = 64
TOPK_MAX = 256
DSA_SPLITS = (DSA_HEADS * HEAD_DIM, DSA_KV_HEADS * HEAD_DIM, DSA_KV_HEADS * HEAD_DIM,
              IDX_HEADS * IDX_DIM, IDX_DIM, IDX_HEADS)
DSA_IN_DIM = sum(DSA_SPLITS)
S5_GROUP = 16
S5_GROUPS = D_MODEL // S5_GROUP
S5_STATE = 64
S5_DT_MIN = 1e-3
S5_DT_MAX = 1e-1
SB_HEADS = 8
SB_HEAD_DIM = D_MODEL // SB_HEADS
FF_DIM = 11 * D_MODEL // 4
N_EXPERTS = 8
TOP_K_EXPERTS = 2
N_A = (DEPTH + 2) // 3
N_B = (DEPTH + 1) // 3
N_C = DEPTH // 3
N_DENSE = (DEPTH + 1) // 2
N_MOE = DEPTH // 2

kernel_name = 'hybrid_dsa_s5_stickbreak_decoder_step'


def rmsnorm(x, g):
    xf = x.astype(jnp.float32)
    y = xf * lax.rsqrt(jnp.mean(xf * xf, axis=-1, keepdims=True) + EPS)
    return (y * g.astype(jnp.float32)).astype(x.dtype)


def gather_pages(pool, layer, page_table):
    g = pool[layer, page_table]
    return g.reshape((g.shape[0], g.shape[1] * g.shape[2]) + g.shape[3:])


def blocked_queries(fn, q_arrays, length):
    n_blocks = -(-length // Q_BLOCK)
    padded = n_blocks * Q_BLOCK
    blocks = []
    for a in q_arrays:
        a = jnp.pad(a, [(0, 0), (0, padded - length)] + [(0, 0)] * (a.ndim - 2))
        blocks.append(jnp.swapaxes(a.reshape((a.shape[0], n_blocks, Q_BLOCK) + a.shape[2:]), 0, 1))
    pos = jnp.arange(padded, dtype=jnp.int32).reshape(n_blocks, Q_BLOCK)
    out = lax.map(lambda args: fn(*args), tuple(blocks) + (pos,))
    out = jnp.swapaxes(out, 0, 1)
    return out.reshape((out.shape[0], padded) + out.shape[3:])[:, :length]


def dsa_project(h, w_in, q_gain, k_gain):
    b, t, _ = h.shape
    offs = [int(o) for o in np.cumsum(DSA_SPLITS)[:-1]]
    q, k, v, qi, ki, wi = jnp.split(h @ w_in, offs, axis=-1)
    q = rmsnorm(q.reshape(b, t, DSA_HEADS, HEAD_DIM), q_gain)
    k = rmsnorm(k.reshape(b, t, DSA_KV_HEADS, HEAD_DIM), k_gain)
    v = v.reshape(b, t, DSA_KV_HEADS, HEAD_DIM)
    qi = qi.reshape(b, t, IDX_HEADS, IDX_DIM)
    return q, k, v, qi, ki, wi


def dsa_block(q, qi, wi, qpos, k, v, ki, topk):
    b, nq = q.shape[:2]
    kpos = jnp.arange(k.shape[1], dtype=jnp.int32)
    causal = kpos[None, :] <= qpos[:, None]
    logits = jnp.einsum('bqhi,bli->bqhl', qi.astype(jnp.float32), ki.astype(jnp.float32))
    score = jnp.einsum('bqh,bqhl->bql', wi.astype(jnp.float32), jax.nn.relu(logits))
    score = jnp.where(causal[None], score, -jnp.inf)
    _, idx = lax.top_k(score, topk)
    valid = idx <= qpos[None, :, None]
    kg = jax.vmap(lambda kb, ib: kb[ib])(k, idx)
    vg = jax.vmap(lambda vb, ib: vb[ib])(v, idx)
    qg = q.reshape(b, nq, DSA_KV_HEADS, DSA_HEADS // DSA_KV_HEADS, HEAD_DIM)
    s = jnp.einsum('bqngd,bqknd->bqngk', qg.astype(jnp.float32), kg.astype(jnp.float32)) / math.sqrt(HEAD_DIM)
    s = jnp.where(valid[:, :, None, None, :], s, -jnp.inf)
    p = jax.nn.softmax(s, axis=-1).astype(v.dtype)
    o = jnp.einsum('bqngk,bqknd->bqngd', p, vg)
    return o.reshape(b, nq, DSA_HEADS * HEAD_DIM)


def sb_project(h, w_in):
    b, t, _ = h.shape
    q, k, v = jnp.split(h @ w_in, 3, axis=-1)
    shp = (b, t, SB_HEADS, SB_HEAD_DIM)
    return q.reshape(shp), k.reshape(shp), v.reshape(shp)


def sb_block(q, qpos, k, v):
    b, nq = q.shape[:2]
    kpos = jnp.arange(k.shape[1], dtype=jnp.int32)
    strict = (kpos[None, :] < qpos[:, None])[None, None]
    z = jnp.einsum('bqhd,blhd->bhql', q.astype(jnp.float32), k.astype(jnp.float32)) / math.sqrt(SB_HEAD_DIM)
    log_keep = jnp.where(strict, jax.nn.log_sigmoid(-z), 0.0)
    suffix = lax.cumsum(log_keep, axis=3, reverse=True) - log_keep
    w = jnp.where(strict, jnp.exp(jax.nn.log_sigmoid(z) + suffix), 0.0)
    o = jnp.einsum('bhql,blhd->bqhd', w.astype(v.dtype), v)
    return o.reshape(b, nq, SB_HEADS * SB_HEAD_DIM)


def _complex_affine_combine(left, right):
    a1r, a1i, b1r, b1i = left
    a2r, a2i, b2r, b2i = right
    return (a2r * a1r - a2i * a1i,
            a2r * a1i + a2i * a1r,
            a2r * b1r - a2i * b1i + b2r,
            a2r * b1i + a2i * b1r + b2i)


def s5_scan(u, h0_re, h0_im, log_dt, a_re, a_im, b_re, b_im, c_re, c_im, d_skip):
    out_dtype = u.dtype
    uf = u.astype(jnp.float32)
    bsz, t, _ = u.shape
    ug = uf.reshape(bsz, t, S5_GROUPS, S5_GROUP)
    a_re = a_re.astype(jnp.float32)
    a_im = a_im.astype(jnp.float32)
    dt = jnp.exp(log_dt.astype(jnp.float32))[:, None]
    decay = jnp.exp(dt * a_re)
    ang = dt * a_im
    ab_re = decay * jnp.cos(ang)
    ab_im = decay * jnp.sin(ang)
    den = a_re * a_re + a_im * a_im
    nr = ab_re - 1.0
    cf_re = (nr * a_re + ab_im * a_im) / den
    cf_im = (ab_im * a_re - nr * a_im) / den
    bu_re = jnp.einsum('btgc,gpc->btgp', ug, b_re.astype(jnp.float32))
    bu_im = jnp.einsum('btgc,gpc->btgp', ug, b_im.astype(jnp.float32))
    x_re = cf_re * bu_re - cf_im * bu_im
    x_im = cf_re * bu_im + cf_im * bu_re
    h0r = h0_re.astype(jnp.float32)
    h0i = h0_im.astype(jnp.float32)
    x_re = x_re.at[:, 0].add(ab_re * h0r - ab_im * h0i)
    x_im = x_im.at[:, 0].add(ab_re * h0i + ab_im * h0r)
    ar_t = jnp.broadcast_to(ab_re, x_re.shape)
    ai_t = jnp.broadcast_to(ab_im, x_re.shape)
    _, _, h_re, h_im = lax.associative_scan(_complex_affine_combine, (ar_t, ai_t, x_re, x_im), axis=1)
    y = (jnp.einsum('btgp,gcp->btgc', h_re, c_re.astype(jnp.float32))
         - jnp.einsum('btgp,gcp->btgc', h_im, c_im.astype(jnp.float32)))
    y = y.reshape(bsz, t, D_MODEL) + d_skip.astype(jnp.float32) * uf
    return y.astype(out_dtype), h_re[:, -1], h_im[:, -1]


def s5_glu(y, w_out):
    z = jax.nn.gelu(y)
    a, g = jnp.split(z @ w_out, 2, axis=-1)
    return a * jax.nn.sigmoid(g)


def swiglu(h, w_gu, w_down):
    g, u = jnp.split(h @ w_gu, 2, axis=-1)
    return (jax.nn.silu(g) * u) @ w_down


def moe_ffn(h, w_router, b_router, w_gu, w_down):
    logits = (h @ w_router).astype(jnp.float32) + b_router.astype(jnp.float32)
    top_v, top_i = lax.top_k(logits, TOP_K_EXPERTS)
    gates = jax.nn.softmax(top_v, axis=-1)
    comb = jnp.einsum('...k,...ke->...e', gates, jax.nn.one_hot(top_i, N_EXPERTS, dtype=jnp.float32))
    comb = comb.astype(h.dtype)
    out = jnp.zeros_like(h)
    for e in range(N_EXPERTS):
        out = out + comb[..., e:e + 1] * swiglu(h, w_gu[e], w_down[e])
    return out


def setup_inputs(seed: int = 0) -> dict:
    key = jax.random.key(seed)
    ks = iter(jax.random.split(key, 40))

    def nrm(shape, scale):
        return jax.random.normal(next(ks), shape, jnp.float32) * scale

    n_pages = PAST_LEN // PAGE_SIZE
    n_used = DEC_BATCH * n_pages
    n_phys = n_used + max(1, n_used // 4)
    d_in = D_MODEL ** -0.5
    d_out = 0.5 * D_MODEL ** -0.5
    a_im_init = jnp.broadcast_to(math.pi * jnp.arange(S5_STATE, dtype=jnp.float32), (N_B, S5_GROUPS, S5_STATE))
    return {
        'x_prompt': nrm((BATCH, SEQ, D_MODEL), 1.0),
        'x_sample': nrm((DEC_BATCH, DEC_SEQ, D_MODEL), 1.0),
        'cache_dsa_k': nrm((N_A, n_phys, PAGE_SIZE, DSA_KV_HEADS, HEAD_DIM), 1.0),
        'cache_dsa_v': nrm((N_A, n_phys, PAGE_SIZE, DSA_KV_HEADS, HEAD_DIM), 1.0),
        'cache_dsa_idx_k': nrm((N_A, n_phys, PAGE_SIZE, IDX_DIM), 1.0),
        'state_ssm_re': nrm((N_B, DEC_BATCH, S5_GROUPS, S5_STATE), 0.1),
        'state_ssm_im': nrm((N_B, DEC_BATCH, S5_GROUPS, S5_STATE), 0.1),
        'cache_sb_k': nrm((N_C, n_phys, PAGE_SIZE, SB_HEADS, SB_HEAD_DIM), 1.0),
        'cache_sb_v': nrm((N_C, n_phys, PAGE_SIZE, SB_HEADS, SB_HEAD_DIM), 1.0),
        'page_table': jax.random.permutation(next(ks), n_phys)[:n_used].reshape(DEC_BATCH, n_pages).astype(jnp.int32),
        'meta_tokens': nrm((N_META, D_MODEL), 1.0),
        'g_mix': 1.0 + nrm((DEPTH, D_MODEL), 0.02),
        'g_ffn': 1.0 + nrm((DEPTH, D_MODEL), 0.02),
        'dsa_w_in': nrm((N_A, D_MODEL, DSA_IN_DIM), d_in),
        'dsa_q_norm': 1.0 + nrm((N_A, HEAD_DIM), 0.02),
        'dsa_k_norm': 1.0 + nrm((N_A, HEAD_DIM), 0.02),
        'dsa_w_out': nrm((N_A, DSA_HEADS * HEAD_DIM, D_MODEL), d_out),
        's5_log_dt': jax.random.uniform(next(ks), (N_B, S5_GROUPS), jnp.float32, math.log(S5_DT_MIN), math.log(S5_DT_MAX)),
        's5_a_re': -0.5 * (1.0 + nrm((N_B, S5_GROUPS, S5_STATE), 0.02)),
        's5_a_im': a_im_init,
        's5_b_re': nrm((N_B, S5_GROUPS, S5_STATE, S5_GROUP), (2 * S5_GROUP) ** -0.5),
        's5_b_im': nrm((N_B, S5_GROUPS, S5_STATE, S5_GROUP), (2 * S5_GROUP) ** -0.5),
        's5_c_re': nrm((N_B, S5_GROUPS, S5_GROUP, S5_STATE), S5_STATE ** -0.5),
        's5_c_im': nrm((N_B, S5_GROUPS, S5_GROUP, S5_STATE), S5_STATE ** -0.5),
        's5_d': nrm((N_B, D_MODEL), 1.0),
        's5_w_out': nrm((N_B, D_MODEL, 2 * D_MODEL), d_in),
        'sb_w_in': nrm((N_C, D_MODEL, 3 * D_MODEL), d_in),
        'sb_w_out': nrm((N_C, D_MODEL, D_MODEL), d_out),
        'ffn_w_gu': nrm((N_DENSE, D_MODEL, 2 * FF_DIM), d_in),
        'ffn_w_down': nrm((N_DENSE, FF_DIM, D_MODEL), 0.5 * FF_DIM ** -0.5),
        'moe_w_router': nrm((N_MOE, D_MODEL, N_EXPERTS), d_in),
        'moe_b_router': nrm((N_MOE, N_EXPERTS), 0.01),
        'moe_w_gu': nrm((N_MOE, N_EXPERTS, D_MODEL, 2 * FF_DIM), d_in),
        'moe_w_down': nrm((N_MOE, N_EXPERTS, FF_DIM, D_MODEL), 0.5 * FF_DIM ** -0.5),
    }


def reference(x_prompt, x_sample, cache_dsa_k, cache_dsa_v, cache_dsa_idx_k, state_ssm_re, state_ssm_im,
              cache_sb_k, cache_sb_v, page_table, meta_tokens, g_mix, g_ffn,
              dsa_w_in, dsa_q_norm, dsa_k_norm, dsa_w_out,
              s5_log_dt, s5_a_re, s5_a_im, s5_b_re, s5_b_im, s5_c_re, s5_c_im, s5_d, s5_w_out,
              sb_w_in, sb_w_out, ffn_w_gu, ffn_w_down, moe_w_router, moe_b_router, moe_w_gu, moe_w_down):
    bsz = x_prompt.shape[0]
    t_p = SEQ + N_META
    topk_p = min(TOPK_MAX, SEQ // 4)
    topk_s = min(TOPK_MAX, (PAST_LEN + DEC_SEQ) // 4)
    qpos_s = PAST_LEN + jnp.arange(DEC_SEQ, dtype=jnp.int32)

    meta = jnp.broadcast_to(meta_tokens[None].astype(x_prompt.dtype), (bsz, N_META, D_MODEL))
    xp = jnp.concatenate([meta, x_prompt], axis=1)
    xs = x_sample

    dsa_k_p, dsa_k_s, dsa_v_p, dsa_v_s, dsa_i_p, dsa_i_s = [], [], [], [], [], []
    ssm_re_p, ssm_re_s, ssm_im_p, ssm_im_s = [], [], [], []
    sb_k_p, sb_k_s, sb_v_p, sb_v_s = [], [], [], []
    ia = ib = ic = i_dense = i_moe = 0
    for layer in range(DEPTH):
        hp = rmsnorm(xp, g_mix[layer])
        hs = rmsnorm(xs, g_mix[layer])
        kind = layer % N_MIXERS
        if kind == 0:
            qp, kp, vp, qip, kip, wip = dsa_project(hp, dsa_w_in[ia], dsa_q_norm[ia], dsa_k_norm[ia])
            op = blocked_queries(lambda q, qi, wi, pos: dsa_block(q, qi, wi, pos, kp, vp, kip, topk_p),
                                 (qp, qip, wip), t_p)
            qs, k_new, v_new, qis, ki_new, wis = dsa_project(hs, dsa_w_in[ia], dsa_q_norm[ia], dsa_k_norm[ia])
            k_all = jnp.concatenate([gather_pages(cache_dsa_k, ia, page_table).astype(k_new.dtype), k_new], axis=1)
            v_all = jnp.concatenate([gather_pages(cache_dsa_v, ia, page_table).astype(v_new.dtype), v_new], axis=1)
            ki_all = jnp.concatenate([gather_pages(cache_dsa_idx_k, ia, page_table).astype(ki_new.dtype), ki_new], axis=1)
            o_s = dsa_block(qs, qis, wis, qpos_s, k_all, v_all, ki_all, topk_s)
            xp = xp + op @ dsa_w_out[ia]
            xs = xs + o_s @ dsa_w_out[ia]
            dsa_k_p.append(kp); dsa_v_p.append(vp); dsa_i_p.append(kip)
            dsa_k_s.append(k_new); dsa_v_s.append(v_new); dsa_i_s.append(ki_new)
            ia += 1
        elif kind == 1:
            zeros = jnp.zeros((bsz, S5_GROUPS, S5_STATE), jnp.float32)
            s5_args = (s5_log_dt[ib], s5_a_re[ib], s5_a_im[ib], s5_b_re[ib], s5_b_im[ib],
                       s5_c_re[ib], s5_c_im[ib], s5_d[ib])
            yp, hre_p, him_p = s5_scan(hp, zeros, zeros, *s5_args)
            ys, hre_s, him_s = s5_scan(hs, state_ssm_re[ib], state_ssm_im[ib], *s5_args)
            xp = xp + s5_glu(yp, s5_w_out[ib])
            xs = xs + s5_glu(ys, s5_w_out[ib])
            ssm_re_p.append(hre_p.astype(state_ssm_re.dtype)); ssm_im_p.append(him_p.astype(state_ssm_im.dtype))
            ssm_re_s.append(hre_s.astype(state_ssm_re.dtype)); ssm_im_s.append(him_s.astype(state_ssm_im.dtype))
            ib += 1
        else:
            qp, kp, vp = sb_project(hp, sb_w_in[ic])
            op = blocked_queries(lambda q, pos: sb_block(q, pos, kp, vp), (qp,), t_p)
            qs, k_new, v_new = sb_project(hs, sb_w_in[ic])
            k_all = jnp.concatenate([gather_pages(cache_sb_k, ic, page_table).astype(k_new.dtype), k_new], axis=1)
            v_all = jnp.concatenate([gather_pages(cache_sb_v, ic, page_table).astype(v_new.dtype), v_new], axis=1)
            o_s = sb_block(qs, qpos_s, k_all, v_all)
            xp = xp + op @ sb_w_out[ic]
            xs = xs + o_s @ sb_w_out[ic]
            sb_k_p.append(kp); sb_v_p.append(vp); sb_k_s.append(k_new); sb_v_s.append(v_new)
            ic += 1
        hp = rmsnorm(xp, g_ffn[layer])
        hs = rmsnorm(xs, g_ffn[layer])
        if layer % 2 == 0:
            xp = xp + swiglu(hp, ffn_w_gu[i_dense], ffn_w_down[i_dense])
            xs = xs + swiglu(hs, ffn_w_gu[i_dense], ffn_w_down[i_dense])
            i_dense += 1
        else:
            xp = xp + moe_ffn(hp, moe_w_router[i_moe], moe_b_router[i_moe], moe_w_gu[i_moe], moe_w_down[i_moe])
            xs = xs + moe_ffn(hs, moe_w_router[i_moe], moe_b_router[i_moe], moe_w_gu[i_moe], moe_w_down[i_moe])
            i_moe += 1

    return (xp[:, N_META:], xs,
            jnp.stack(dsa_k_p), jnp.stack(dsa_k_s), jnp.stack(dsa_v_p), jnp.stack(dsa_v_s),
            jnp.stack(dsa_i_p), jnp.stack(dsa_i_s),
            jnp.stack(ssm_re_p), jnp.stack(ssm_re_s), jnp.stack(ssm_im_p), jnp.stack(ssm_im_s),
            jnp.stack(sb_k_p), jnp.stack(sb_k_s), jnp.stack(sb_v_p), jnp.stack(sb_v_s))
```

```python
import functools
import math

import jax
import jax.numpy as jnp
from jax import lax
from jax.experimental import pallas as pl
from jax.experimental.pallas import tpu as pltpu

F32 = jnp.float32
BF16 = jnp.bfloat16
I32 = jnp.int32

D_MODEL = 1024
N_META = 16
EPS = 1e-6
DSA_HEADS = 8
DSA_KV_HEADS = 2
DSA_GROUP = DSA_HEADS // DSA_KV_HEADS
HEAD_DIM = 128
IDX_HEADS = 8
IDX_DIM = 64
TOPK_MAX = 256
S5_GROUP = 16
S5_GROUPS = 64
S5_STATE = 64
S5_NSTATE = S5_GROUPS * S5_STATE
S5_HALF = S5_NSTATE // 2
SB_HEADS = 8
SB_HEAD_DIM = 128
FF_DIM = 11 * D_MODEL // 4
N_EXPERTS = 8
PAGE = 128

LANES = 128
SUBLANES = 8
MXU_DIM = 256
VMEM_LIMIT = 56 * 1024 * 1024

QBLK = 128
PROMPT_TILE = 512
DSA_IN_DIM = (DSA_HEADS + 2 * DSA_KV_HEADS) * HEAD_DIM + IDX_HEADS * IDX_DIM + IDX_DIM + IDX_HEADS
DSA_IN_PAD = -(-DSA_IN_DIM // LANES) * LANES
INT_MIN = -(2 ** 31)
NEG_BIG = -1e30


def _cparams(sem, vmem=None):
    return pltpu.CompilerParams(dimension_semantics=sem, vmem_limit_bytes=vmem)


def _const_spec(shape):
    nd = len(shape)
    return pl.BlockSpec(shape, lambda *_: (0,) * nd, pipeline_mode=pl.Buffered(1))


def _rms(x, g):
    return x * lax.rsqrt(jnp.mean(x * x, axis=-1, keepdims=True) + EPS) * g


def _dot(a, b):
    return jnp.dot(a, b, preferred_element_type=F32)


def _dot_nt(a, b):
    return lax.dot_general(a, b, (((1,), (1,)), ((), ())), preferred_element_type=F32)


def _split_bf16(x):
    hi = x.astype(BF16)
    lo = (x - hi.astype(F32)).astype(BF16)
    return hi, lo


def _dsa_proj_kernel(x_ref, g_ref, w_ref, qg_ref, kg_ref,
                     q_ref, k_ref, v_ref, qi_ref, ki_ref, wi_ref):
    h = _rms(x_ref[...], g_ref[...]).astype(BF16)
    nq = DSA_HEADS * HEAD_DIM
    nkv = DSA_KV_HEADS * HEAD_DIM
    q = _dot(h, w_ref[:, 0:nq])
    scale = 1.0 / math.sqrt(HEAD_DIM)
    for hd in range(DSA_HEADS):
        sl = slice(hd * HEAD_DIM, (hd + 1) * HEAD_DIM)
        q_ref[:, sl] = (_rms(q[:, sl], qg_ref[...]) * scale).astype(BF16)
    k = _dot(h, w_ref[:, nq:nq + nkv])
    for hd in range(DSA_KV_HEADS):
        sl = slice(hd * HEAD_DIM, (hd + 1) * HEAD_DIM)
        k_ref[:, sl] = _rms(k[:, sl], kg_ref[...])
    v_ref[...] = _dot(h, w_ref[:, nq + nkv:nq + 2 * nkv])
    o = nq + 2 * nkv
    qi_ref[...] = _dot(h, w_ref[:, o:o + IDX_HEADS * IDX_DIM]).astype(BF16)
    o += IDX_HEADS * IDX_DIM
    tail = _dot(h, w_ref[:, o:o + LANES])
    ki_ref[...] = tail[:, 0:IDX_DIM]
    wi_ref[...] = tail[:, IDX_DIM:IDX_DIM + IDX_HEADS]


def dsa_project(x, g, w_pad, qg, kg, tm):
    m = x.shape[0]
    nq = DSA_HEADS * HEAD_DIM
    nkv = DSA_KV_HEADS * HEAD_DIM
    row = lambda n: pl.BlockSpec((tm, n), lambda i: (i, 0))
    return pl.pallas_call(
        _dsa_proj_kernel,
        grid=(m // tm,),
        in_specs=[row(D_MODEL), _const_spec((1, D_MODEL)), _const_spec(w_pad.shape),
                  _const_spec((1, HEAD_DIM)), _const_spec((1, HEAD_DIM))],
        out_specs=[row(nq), row(nkv), row(nkv), row(IDX_HEADS * IDX_DIM), row(IDX_DIM), row(IDX_HEADS)],
        out_shape=[jax.ShapeDtypeStruct((m, nq), BF16),
                   jax.ShapeDtypeStruct((m, nkv), F32),
                   jax.ShapeDtypeStruct((m, nkv), F32),
                   jax.ShapeDtypeStruct((m, IDX_HEADS * IDX_DIM), BF16),
                   jax.ShapeDtypeStruct((m, IDX_DIM), F32),
                   jax.ShapeDtypeStruct((m, IDX_HEADS), F32)],
        compiler_params=_cparams(("parallel",), VMEM_LIMIT),
        name="dsa_project",
    )(x, g.reshape(1, D_MODEL), w_pad, qg.reshape(1, HEAD_DIM), kg.reshape(1, HEAD_DIM))


def _sb_proj_kernel(x_ref, g_ref, w_ref, q_ref, k_ref, v_ref):
    h = _rms(x_ref[...], g_ref[...]).astype(BF16)
    scale = 1.0 / math.sqrt(SB_HEAD_DIM)
    q_ref[...] = (_dot(h, w_ref[:, 0:D_MODEL]) * scale).astype(BF16)
    k_ref[...] = _dot(h, w_ref[:, D_MODEL:2 * D_MODEL])
    v_ref[...] = _dot(h, w_ref[:, 2 * D_MODEL:3 * D_MODEL])


def sb_project(x, g, w, tm):
    m = x.shape[0]
    row = lambda n: pl.BlockSpec((tm, n), lambda i: (i, 0))
    return pl.pallas_call(
        _sb_proj_kernel,
        grid=(m // tm,),
        in_specs=[row(D_MODEL), _const_spec((1, D_MODEL)), _const_spec(w.shape)],
        out_specs=[row(D_MODEL)] * 3,
        out_shape=[jax.ShapeDtypeStruct((m, D_MODEL), BF16),
                   jax.ShapeDtypeStruct((m, D_MODEL), F32),
                   jax.ShapeDtypeStruct((m, D_MODEL), F32)],
        compiler_params=_cparams(("parallel",), VMEM_LIMIT),
        name="sb_project",
    )(x, g.reshape(1, D_MODEL), w)


def _out_proj_kernel(x_ref, o_ref, w_ref, y_ref):
    y_ref[...] = x_ref[...] + _dot(o_ref[...], w_ref[...])


def out_project(x, o, w, tm):
    m = x.shape[0]
    row = pl.BlockSpec((tm, D_MODEL), lambda i: (i, 0))
    return pl.pallas_call(
        _out_proj_kernel,
        grid=(m // tm,),
        in_specs=[row, row, _const_spec(w.shape)],
        out_specs=row,
        out_shape=jax.ShapeDtypeStruct((m, D_MODEL), F32),
        compiler_params=_cparams(("parallel",), VMEM_LIMIT),
        name="out_project",
    )(x, o, w)


def _gelu_tanh(y):
    c = math.sqrt(2.0 / math.pi)
    return 0.5 * y * (1.0 + jnp.tanh(c * (y + 0.044715 * (y * y * y))))


def _s5_glu_kernel(x_ref, y_ref, w_ref, o_ref):
    z = _gelu_tanh(y_ref[...]).astype(BF16)
    a = _dot(z, w_ref[:, 0:D_MODEL])
    g = _dot(z, w_ref[:, D_MODEL:2 * D_MODEL])
    o_ref[...] = x_ref[...] + a * jax.nn.sigmoid(g)


def s5_glu(x, y, w, tm):
    m = x.shape[0]
    row = pl.BlockSpec((tm, D_MODEL), lambda i: (i, 0))
    return pl.pallas_call(
        _s5_glu_kernel,
        grid=(m // tm,),
        in_specs=[row, row, _const_spec(w.shape)],
        out_specs=row,
        out_shape=jax.ShapeDtypeStruct((m, D_MODEL), F32),
        compiler_params=_cparams(("parallel",), VMEM_LIMIT),
        name="s5_glu",
    )(x, y, w)


FF_CHUNK = 256


def _swiglu_acc(h, wgu_ref, wd_ref, acc):
    for c in range(FF_DIM // FF_CHUNK):
        lo = c * FF_CHUNK
        g = _dot(h, wgu_ref[:, lo:lo + FF_CHUNK])
        u = _dot(h, wgu_ref[:, FF_DIM + lo:FF_DIM + lo + FF_CHUNK])
        a = (g * jax.nn.sigmoid(g) * u).astype(BF16)
        acc = acc + _dot(a, wd_ref[lo:lo + FF_CHUNK, :])
    return acc


def _ffn_kernel(x_ref, g_ref, wgu_ref, wd_ref, o_ref):
    x = x_ref[...]
    h = _rms(x, g_ref[...]).astype(BF16)
    o_ref[...] = _swiglu_acc(h, wgu_ref, wd_ref, x)


def ffn(x, g, wgu, wd, tm):
    m = x.shape[0]
    row = pl.BlockSpec((tm, D_MODEL), lambda i: (i, 0))
    return pl.pallas_call(
        _ffn_kernel,
        grid=(m // tm,),
        in_specs=[row, _const_spec((1, D_MODEL)), _const_spec(wgu.shape), _const_spec(wd.shape)],
        out_specs=row,
        out_shape=jax.ShapeDtypeStruct((m, D_MODEL), F32),
        compiler_params=_cparams(("parallel",), VMEM_LIMIT),
        name="ffn",
    )(x, g.reshape(1, D_MODEL), wgu, wd)


def _moe_kernel(x_ref, g_ref, wr_ref, br_ref, wgu_ref, wd_ref, o_ref, h_scr, comb_scr):
    e = pl.program_id(1)

    @pl.when(e == 0)
    def _():
        x = x_ref[...]
        hf = _rms(x, g_ref[...])
        h_scr[...] = hf.astype(BF16)
        h_hi, h_lo = _split_bf16(hf)
        w_hi, w_lo = _split_bf16(wr_ref[...])
        logits = _dot(h_hi, w_hi) + _dot(h_lo, w_hi) + _dot(h_hi, w_lo) + br_ref[...]
        lane = lax.broadcasted_iota(I32, logits.shape, 1)
        m1 = jnp.max(logits, axis=-1, keepdims=True)
        i1 = jnp.min(jnp.where(logits == m1, lane, LANES), axis=-1, keepdims=True)
        rest = jnp.where(lane == i1, NEG_BIG, logits)
        m2 = jnp.max(rest, axis=-1, keepdims=True)
        i2 = jnp.min(jnp.where(rest == m2, lane, LANES), axis=-1, keepdims=True)
        e2 = jnp.exp(m2 - m1)
        g1 = 1.0 / (1.0 + e2)
        g2 = e2 / (1.0 + e2)
        comb_scr[...] = jnp.where(lane == i1, g1, 0.0) + jnp.where(lane == i2, g2, 0.0)
        o_ref[...] = x

    lane = lax.broadcasted_iota(I32, comb_scr.shape, 1)
    gate = jnp.sum(jnp.where(lane == e, comb_scr[...], 0.0), axis=-1, keepdims=True)
    y = _swiglu_acc(h_scr[...], wgu_ref.at[0], wd_ref.at[0], jnp.zeros(o_ref.shape, F32))
    o_ref[...] = o_ref[...] + gate * y


def moe(x, g, w_router_pad, b_router_pad, wgu, wd, tm):
    m = x.shape[0]
    row = pl.BlockSpec((tm, D_MODEL), lambda i, e: (i, 0))
    cst = lambda shape: pl.BlockSpec(shape, lambda i, e: (0,) * len(shape), pipeline_mode=pl.Buffered(1))
    return pl.pallas_call(
        _moe_kernel,
        grid=(m // tm, N_EXPERTS),
        in_specs=[row, cst((1, D_MODEL)), cst((D_MODEL, LANES)), cst((1, LANES)),
                  pl.BlockSpec((1, D_MODEL, 2 * FF_DIM), lambda i, e: (e, 0, 0)),
                  pl.BlockSpec((1, FF_DIM, D_MODEL), lambda i, e: (e, 0, 0))],
        out_specs=row,
        out_shape=jax.ShapeDtypeStruct((m, D_MODEL), F32),
        scratch_shapes=[pltpu.VMEM((tm, D_MODEL), BF16), pltpu.VMEM((tm, LANES), F32)],
        compiler_params=_cparams(("parallel", "arbitrary"), VMEM_LIMIT),
        name="moe",
    )(x, g.reshape(1, D_MODEL), w_router_pad, b_router_pad, wgu, wd)


def _sortable_key(s):
    bits = pltpu.bitcast(s, I32)
    return bits ^ ((bits >> 31) & 0x7FFFFFFF)


def _kth_largest_key(count_ge, topk, shape):
    zero = jnp.zeros(shape, I32)
    t0 = jnp.where(count_ge(zero) >= topk, zero, jnp.full(shape, INT_MIN, I32))

    def body(k, t):
        cand = t | jnp.left_shift(jnp.int32(1), 30 - k)
        return jnp.where(count_ge(cand) >= topk, cand, t)

    t = lax.fori_loop(0, 31, body, t0)
    return jnp.maximum(t, INT_MIN + 1)


def _tie_prefix_matrix():
    r = lax.broadcasted_iota(I32, (QBLK, QBLK), 0)
    c = lax.broadcasted_iota(I32, (QBLK, QBLK), 1)
    return jnp.where(r < c, 1.0, 0.0).astype(BF16)


def _select_chunk(key, t, need, carry, upper):
    eq = key == t
    eqf = jnp.where(eq, 1.0, 0.0)
    before = _dot(eqf.astype(BF16), upper) + carry
    sel = (key > t) | (eq & (before < need))
    return sel, carry + jnp.sum(eqf, axis=-1, keepdims=True)


def _softmax_update(s, sel, v, m_ref, l_ref, acc_ref, heads, rows):
    ps = []
    alphas = []
    for j, hd in enumerate(heads):
        sj = s[j * rows:(j + 1) * rows]
        m_old = m_ref[hd]
        m_new = jnp.maximum(m_old, jnp.max(jnp.where(sel, sj, NEG_BIG), axis=-1, keepdims=True))
        p = jnp.where(sel, jnp.exp(sj - m_new), 0.0)
        alpha = jnp.exp(m_old - m_new)
        l_ref[hd] = alpha * l_ref[hd] + jnp.sum(p, axis=-1, keepdims=True)
        m_ref[hd] = m_new
        ps.append(p.astype(BF16))
        alphas.append(alpha)
    pv = _dot(jnp.concatenate(ps, axis=0), v)
    for j, hd in enumerate(heads):
        acc_ref[hd] = alphas[j] * acc_ref[hd] + pv[j * rows:(j + 1) * rows]


def _dsa_prompt_kernel(q_ref, qi_ref, wi_ref, k_ref, v_ref, ki_ref, o_ref,
                       skey_ref, wib_ref, t_ref, need_ref, m_ref, l_ref, acc_ref, *, topk):
    i = pl.program_id(1)
    nchunk = i + 1
    shape = (QBLK, QBLK)
    row_pos = i * QBLK + lax.broadcasted_iota(I32, shape, 0)
    col_iota = lax.broadcasted_iota(I32, shape, 1)

    qi = qi_ref[...]
    qih = jnp.concatenate([qi[:, h * IDX_DIM:(h + 1) * IDX_DIM] for h in range(IDX_HEADS)], axis=0)
    wi = wi_ref[...]
    for h in range(IDX_HEADS):
        wib_ref[h] = jnp.broadcast_to(wi[:, h:h + 1], shape)

    def score_chunk(c, _):
        start = pl.multiple_of(c * QBLK, QBLK)
        kic = ki_ref[pl.ds(start, QBLK), :].astype(BF16)
        lg = _dot_nt(qih, kic)
        s = jnp.zeros(shape, F32)
        for h in range(IDX_HEADS):
            s = s + wib_ref[h] * jnp.maximum(lg[h * QBLK:(h + 1) * QBLK], 0.0)
        key = jnp.where(c * QBLK + col_iota <= row_pos, _sortable_key(s), INT_MIN)
        skey_ref[c] = key
        return 0

    lax.fori_loop(0, nchunk, score_chunk, 0)

    def count_ge(cand):
        def body(c, acc):
            return acc + jnp.where(skey_ref[c] >= cand, 1.0, 0.0)
        acc = lax.fori_loop(0, nchunk, body, jnp.zeros(shape, F32))
        return jnp.sum(acc, axis=-1, keepdims=True)

    @pl.when(nchunk * QBLK <= topk)
    def _():
        t_ref[...] = jnp.full(shape, INT_MIN + 1, I32)
        need_ref[...] = jnp.zeros(shape, F32)

    @pl.when(nchunk * QBLK > topk)
    def _():
        t = _kth_largest_key(count_ge, float(topk), shape)
        t_ref[...] = t
        need_ref[...] = jnp.broadcast_to(float(topk) - count_ge(t + 1), shape)

    q = q_ref[...]
    qn = [jnp.concatenate([q[:, (n * DSA_GROUP + g) * HEAD_DIM:(n * DSA_GROUP + g + 1) * HEAD_DIM]
                           for g in range(DSA_GROUP)], axis=0) for n in range(DSA_KV_HEADS)]
    m_ref[...] = jnp.full(m_ref.shape, NEG_BIG, F32)
    l_ref[...] = jnp.zeros(l_ref.shape, F32)
    acc_ref[...] = jnp.zeros(acc_ref.shape, F32)
    upper = _tie_prefix_matrix()
    t = t_ref[...]
    need = need_ref[...]

    def attn_chunk(c, carry):
        start = pl.multiple_of(c * QBLK, QBLK)
        sel, carry = _select_chunk(skey_ref[c], t, need, carry, upper)
        kc = k_ref[pl.ds(start, QBLK), :].astype(BF16)
        vc = v_ref[pl.ds(start, QBLK), :].astype(BF16)
        for n in range(DSA_KV_HEADS):
            sl = slice(n * HEAD_DIM, (n + 1) * HEAD_DIM)
            s = _dot_nt(qn[n], kc[:, sl])
            heads = [n * DSA_GROUP + g for g in range(DSA_GROUP)]
            _softmax_update(s, sel, vc[:, sl], m_ref, l_ref, acc_ref, heads, QBLK)
        return carry

    lax.fori_loop(0, nchunk, attn_chunk, jnp.zeros(shape, F32))
    for h in range(DSA_HEADS):
        o_ref[:, h * HEAD_DIM:(h + 1) * HEAD_DIM] = (acc_ref[h] / l_ref[h]).astype(BF16)


def dsa_attention_prompt(q, qi, wi, k, v, ki, topk):
    b, t, _ = q.shape
    nq = t // QBLK
    blk = lambda n: pl.BlockSpec((None, QBLK, n), lambda bi, i: (bi, i, 0))
    full = lambda n: pl.BlockSpec((None, t, n), lambda bi, i: (bi, 0, 0))
    hshape = (DSA_HEADS, QBLK, QBLK)
    return pl.pallas_call(
        functools.partial(_dsa_prompt_kernel, topk=topk),
        grid=(b, nq),
        in_specs=[blk(q.shape[2]), blk(qi.shape[2]), blk(wi.shape[2]),
                  full(k.shape[2]), full(v.shape[2]), full(ki.shape[2])],
        out_specs=blk(q.shape[2]),
        out_shape=jax.ShapeDtypeStruct(q.shape, BF16),
        scratch_shapes=[pltpu.VMEM((nq, QBLK, QBLK), I32),
                        pltpu.VMEM((IDX_HEADS, QBLK, QBLK), F32),
                        pltpu.VMEM((QBLK, QBLK), I32),
                        pltpu.VMEM((QBLK, QBLK), F32),
                        pltpu.VMEM(hshape, F32), pltpu.VMEM(hshape, F32), pltpu.VMEM(hshape, F32)],
        compiler_params=_cparams(("parallel", "arbitrary"), VMEM_LIMIT),
        name="dsa_attention_prompt",
    )(q, qi, wi, k, v, ki)


PAGES_PER_STEP = 4
QPAD = SUBLANES


def _page_specs(n_pages, width, layer, order):
    def spec(r):
        return pl.BlockSpec((None, None, PAGE, width),
                            lambda b, j, pt: (layer, pt[b, order(j, r)], 0, 0))
    return [spec(r) for r in range(PAGES_PER_STEP)]


def _dsa_sample_select_kernel(pt_ref, qi_ref, wi_ref, *refs, n_pages, n_new, topk):
    ki_refs = refs[:PAGES_PER_STEP]
    kin_ref, sel_ref, skey_ref, wib_ref, qih_ref = refs[PAGES_PER_STEP:]
    j = pl.program_id(1)
    shape = (QPAD, LANES)

    @pl.when(j == 0)
    def _():
        qi = qi_ref[...].astype(F32)
        qih_ref[...] = jnp.concatenate(
            [qi[:, h * IDX_DIM:(h + 1) * IDX_DIM] for h in range(IDX_HEADS)], axis=0).astype(BF16)
        wi = wi_ref[...]
        for h in range(IDX_HEADS):
            wib_ref[h] = jnp.broadcast_to(wi[:, h:h + 1], shape)

    def score(kic):
        lg = _dot_nt(qih_ref[...], kic.astype(BF16))
        s = jnp.zeros(shape, F32)
        for h in range(IDX_HEADS):
            s = s + wib_ref[h] * jnp.maximum(lg[h * QPAD:(h + 1) * QPAD], 0.0)
        return _sortable_key(s)

    for r in range(PAGES_PER_STEP):
        skey_ref[j * PAGES_PER_STEP + r] = score(ki_refs[r][...])

    @pl.when(j == pl.num_programs(1) - 1)
    def _():
        row = lax.broadcasted_iota(I32, shape, 0)
        col = lax.broadcasted_iota(I32, shape, 1)
        visible = (col <= row) & (row < n_new)
        skey_ref[n_pages] = jnp.where(visible, score(kin_ref[...]), INT_MIN)
        nchunk = n_pages + 1

        def count_ge(cand):
            def body(c, acc):
                return acc + jnp.where(skey_ref[c] >= cand, 1.0, 0.0)
            acc = lax.fori_loop(0, nchunk, body, jnp.zeros(shape, F32))
            return jnp.sum(acc, axis=-1, keepdims=True)

        t = _kth_largest_key(count_ge, float(topk), shape)
        need = jnp.broadcast_to(float(topk) - count_ge(t + 1), shape)
        upper = _tie_prefix_matrix()

        def emit(c, carry):
            sel, carry = _select_chunk(skey_ref[c], t, need, carry, upper)
            sel_ref[c] = jnp.where(sel, 1.0, 0.0)
            return carry

        lax.fori_loop(0, nchunk, emit, jnp.zeros(shape, F32))


def dsa_sample_select(page_table, qi8, wi8, cache_idx_k, layer, ki_new_pad, n_new, topk):
    nb, n_pages = page_table.shape
    nsteps = n_pages // PAGES_PER_STEP
    per_b = lambda shape: pl.BlockSpec((None,) + shape, lambda b, j, pt: (b,) + (0,) * len(shape))
    grid_spec = pltpu.PrefetchScalarGridSpec(
        num_scalar_prefetch=1,
        grid=(nb, nsteps),
        in_specs=[per_b((QPAD, IDX_HEADS * IDX_DIM)), per_b((QPAD, IDX_HEADS))]
        + _page_specs(n_pages, IDX_DIM, layer, lambda j, r: j * PAGES_PER_STEP + r)
        + [per_b((PAGE, IDX_DIM))],
        out_specs=per_b((n_pages + 1, QPAD, LANES)),
        scratch_shapes=[pltpu.VMEM((n_pages + 1, QPAD, LANES), I32),
                        pltpu.VMEM((IDX_HEADS, QPAD, LANES), F32),
                        pltpu.VMEM((IDX_HEADS * QPAD, IDX_DIM), BF16)],
    )
    return pl.pallas_call(
        functools.partial(_dsa_sample_select_kernel, n_pages=n_pages, n_new=n_new, topk=topk),
        grid_spec=grid_spec,
        out_shape=jax.ShapeDtypeStruct((nb, n_pages + 1, QPAD, LANES), F32),
        compiler_params=_cparams(("parallel", "arbitrary"), VMEM_LIMIT),
        name="dsa_sample_select",
    )(page_table, qi8, wi8, *([cache_idx_k] * PAGES_PER_STEP), ki_new_pad)


def _dsa_sample_attn_kernel(pt_ref, q_ref, sel_ref, *refs, n_pages):
    k_refs = refs[:PAGES_PER_STEP]
    v_refs = refs[PAGES_PER_STEP:2 * PAGES_PER_STEP]
    kn_ref, vn_ref, o_ref, qn_ref, m_ref, l_ref, acc_ref = refs[2 * PAGES_PER_STEP:]
    j = pl.program_id(1)

    @pl.when(j == 0)
    def _():
        q = q_ref[...].astype(F32)
        for n in range(DSA_KV_HEADS):
            qn_ref[n] = jnp.concatenate(
                [q[:, (n * DSA_GROUP + g) * HEAD_DIM:(n * DSA_GROUP + g + 1) * HEAD_DIM]
                 for g in range(DSA_GROUP)], axis=0).astype(BF16)
        m_ref[...] = jnp.full(m_ref.shape, NEG_BIG, F32)
        l_ref[...] = jnp.zeros(l_ref.shape, F32)
        acc_ref[...] = jnp.zeros(acc_ref.shape, F32)

    def chunk(c, k, v):
        sel = sel_ref[c] > 0.5
        kc = k.astype(BF16)
        vc = v.astype(BF16)
        for n in range(DSA_KV_HEADS):
            sl = slice(n * HEAD_DIM, (n + 1) * HEAD_DIM)
            s = _dot_nt(qn_ref[n], kc[:, sl])
            heads = [n * DSA_GROUP + g for g in range(DSA_GROUP)]
            _softmax_update(s, sel, vc[:, sl], m_ref, l_ref, acc_ref, heads, QPAD)

    for r in range(PAGES_PER_STEP):
        chunk(j * PAGES_PER_STEP + r, k_refs[r][...], v_refs[r][...])

    @pl.when(j == pl.num_programs(1) - 1)
    def _():
        chunk(n_pages, kn_ref[...], vn_ref[...])
        for h in range(DSA_HEADS):
            o_ref[:, h * HEAD_DIM:(h + 1) * HEAD_DIM] = acc_ref[h] / l_ref[h]


def dsa_sample_attention(page_table, q8, sel, cache_k, cache_v, layer, k_new_pad, v_new_pad):
    nb, n_pages = page_table.shape
    nsteps = n_pages // PAGES_PER_STEP
    width = DSA_KV_HEADS * HEAD_DIM
    per_b = lambda shape: pl.BlockSpec((None,) + shape, lambda b, j, pt: (b,) + (0,) * len(shape))
    order = lambda j, r: j * PAGES_PER_STEP + r
    hshape = (DSA_HEADS, QPAD, LANES)
    grid_spec = pltpu.PrefetchScalarGridSpec(
        num_scalar_prefetch=1,
        grid=(nb, nsteps),
        in_specs=[per_b((QPAD, DSA_HEADS * HEAD_DIM)), per_b((n_pages + 1, QPAD, LANES))]
        + _page_specs(n_pages, width, layer, order) + _page_specs(n_pages, width, layer, order)
        + [per_b((PAGE, width)), per_b((PAGE, width))],
        out_specs=per_b((QPAD, DSA_HEADS * HEAD_DIM)),
        scratch_shapes=[pltpu.VMEM((DSA_KV_HEADS, DSA_GROUP * QPAD, HEAD_DIM), BF16),
                        pltpu.VMEM(hshape, F32), pltpu.VMEM(hshape, F32), pltpu.VMEM(hshape, F32)],
    )
    return pl.pallas_call(
        functools.partial(_dsa_sample_attn_kernel, n_pages=n_pages),
        grid_spec=grid_spec,
        out_shape=jax.ShapeDtypeStruct((nb, QPAD, DSA_HEADS * HEAD_DIM), F32),
        compiler_params=_cparams(("parallel", "arbitrary"), VMEM_LIMIT),
        name="dsa_sample_attention",
    )(page_table, q8, sel, *([cache_k] * PAGES_PER_STEP), *([cache_v] * PAGES_PER_STEP),
      k_new_pad, v_new_pad)


def _sb_sample_kernel(pt_ref, q_ref, *refs, n_new):
    k_refs = refs[:PAGES_PER_STEP]
    v_refs = refs[PAGES_PER_STEP:2 * PAGES_PER_STEP]
    kn_ref, vn_ref, o_ref, qbd_ref, carry_ref, acc_ref = refs[2 * PAGES_PER_STEP:]
    j = pl.program_id(1)
    suffix_mat = _suffix_matrix()
    rows = SB_HEADS * QPAD

    @pl.when(j == 0)
    def _():
        q = q_ref[...].astype(F32)
        head_of_col = lax.broadcasted_iota(I32, q.shape, 1) // SB_HEAD_DIM
        qbd_ref[...] = jnp.concatenate(
            [jnp.where(head_of_col == h, q, 0.0) for h in range(SB_HEADS)], axis=0).astype(BF16)
        row = lax.broadcasted_iota(I32, (rows, LANES), 0) % QPAD
        col = lax.broadcasted_iota(I32, (rows, LANES), 1)
        strict = (col < row) & (row < n_new)
        carry, acc = _sb_chunk(qbd_ref[...], kn_ref[...].astype(BF16), vn_ref[...].astype(BF16),
                               jnp.zeros((rows, LANES), F32), jnp.zeros(acc_ref.shape, F32),
                               suffix_mat, strict)
        carry_ref[...] = carry
        acc_ref[...] = acc

    for r in range(PAGES_PER_STEP):
        carry, acc = _sb_chunk(qbd_ref[...], k_refs[r][...].astype(BF16), v_refs[r][...].astype(BF16),
                               carry_ref[...], acc_ref[...], suffix_mat, None)
        carry_ref[...] = carry
        acc_ref[...] = acc

    @pl.when(j == pl.num_programs(1) - 1)
    def _():
        for h in range(SB_HEADS):
            sl = slice(h * SB_HEAD_DIM, (h + 1) * SB_HEAD_DIM)
            o_ref[:, sl] = acc_ref[h * QPAD:(h + 1) * QPAD, sl]


def sb_sample_attention(page_table, q8, cache_k, cache_v, layer, k_new_pad, v_new_pad, n_new):
    nb, n_pages = page_table.shape
    nsteps = n_pages // PAGES_PER_STEP
    width = SB_HEADS * SB_HEAD_DIM
    per_b = lambda shape: pl.BlockSpec((None,) + shape, lambda b, j, pt: (b,) + (0,) * len(shape))
    order = lambda j, r: n_pages - 1 - (j * PAGES_PER_STEP + r)
    grid_spec = pltpu.PrefetchScalarGridSpec(
        num_scalar_prefetch=1,
        grid=(nb, nsteps),
        in_specs=[per_b((QPAD, width))]
        + _page_specs(n_pages, width, layer, order) + _page_specs(n_pages, width, layer, order)
        + [per_b((PAGE, width)), per_b((PAGE, width))],
        out_specs=per_b((QPAD, width)),
        scratch_shapes=[pltpu.VMEM((SB_HEADS * QPAD, width), BF16),
                        pltpu.VMEM((SB_HEADS * QPAD, LANES), F32),
                        pltpu.VMEM((SB_HEADS * QPAD, width), F32)],
    )
    return pl.pallas_call(
        functools.partial(_sb_sample_kernel, n_new=n_new),
        grid_spec=grid_spec,
        out_shape=jax.ShapeDtypeStruct((nb, QPAD, width), F32),
        compiler_params=_cparams(("parallel", "arbitrary"), VMEM_LIMIT),
        name="sb_sample_attention",
    )(page_table, q8, *([cache_k] * PAGES_PER_STEP), *([cache_v] * PAGES_PER_STEP),
      k_new_pad, v_new_pad)


def _pad_rows(a, nb, n_new, rows):
    a = a.reshape(nb, n_new, a.shape[-1])
    return jnp.pad(a, ((0, 0), (0, rows - n_new), (0, 0)))


def dsa_sample(page_table, q, qi, wi, k_new, v_new, ki_new, cache_k, cache_v, cache_idx_k, layer, topk):
    nb = page_table.shape[0]
    n_new = q.shape[0] // nb
    sel = dsa_sample_select(page_table, _pad_rows(qi, nb, n_new, QPAD), _pad_rows(wi, nb, n_new, QPAD),
                            cache_idx_k, layer, _pad_rows(ki_new, nb, n_new, PAGE), n_new, topk)
    kv = lambda c: c.reshape(c.shape[:3] + (DSA_KV_HEADS * HEAD_DIM,))
    o = dsa_sample_attention(page_table, _pad_rows(q, nb, n_new, QPAD), sel, kv(cache_k), kv(cache_v), layer,
                             _pad_rows(k_new, nb, n_new, PAGE), _pad_rows(v_new, nb, n_new, PAGE))
    return o[:, :n_new].reshape(nb * n_new, -1)


def sb_sample(page_table, q, k_new, v_new, cache_k, cache_v, layer):
    nb = page_table.shape[0]
    n_new = q.shape[0] // nb
    kv = lambda c: c.reshape(c.shape[:3] + (SB_HEADS * SB_HEAD_DIM,))
    o = sb_sample_attention(page_table, _pad_rows(q, nb, n_new, QPAD), kv(cache_k), kv(cache_v), layer,
                            _pad_rows(k_new, nb, n_new, PAGE), _pad_rows(v_new, nb, n_new, PAGE), n_new)
    return o[:, :n_new].reshape(nb * n_new, -1)


def _suffix_matrix():
    r = lax.broadcasted_iota(I32, (QBLK, 2 * QBLK), 0)
    c = lax.broadcasted_iota(I32, (QBLK, 2 * QBLK), 1)
    return jnp.where((r > c) | (c >= QBLK), 1.0, 0.0).astype(BF16)


def _sb_chunk(q, kc, vc, carry, acc, suffix_mat, strict):
    rows = q.shape[0]
    z = _dot_nt(q, kc)
    lk = -(jnp.maximum(z, 0.0) + jnp.log(1.0 + jnp.exp(-jnp.abs(z))))
    lkm = lk if strict is None else jnp.where(strict, lk, 0.0)
    hi, lo = _split_bf16(lkm)
    r2 = _dot(jnp.concatenate([hi, lo], axis=0), suffix_mat)
    r2 = r2[:rows] + r2[rows:]
    w = jnp.exp(z + lk + r2[:, :QBLK] + carry)
    if strict is not None:
        w = jnp.where(strict, w, 0.0)
    return carry + r2[:, QBLK:], acc + _dot(w.astype(BF16), vc)


def _sb_prompt_kernel(q_ref, k_ref, v_ref, o_ref):
    i = pl.program_id(2)
    q = q_ref[...]
    suffix_mat = _suffix_matrix()
    shape = (QBLK, QBLK)
    strict = lax.broadcasted_iota(I32, shape, 1) < lax.broadcasted_iota(I32, shape, 0)

    def load(c):
        start = pl.multiple_of(c * QBLK, QBLK)
        return k_ref[pl.ds(start, QBLK), :].astype(BF16), v_ref[pl.ds(start, QBLK), :].astype(BF16)

    kc, vc = load(i)
    zero = jnp.zeros(shape, F32)
    carry, acc = _sb_chunk(q, kc, vc, zero, zero, suffix_mat, strict)

    def body(j, state):
        kc, vc = load(i - 1 - j)
        return _sb_chunk(q, kc, vc, state[0], state[1], suffix_mat, None)

    _, acc = lax.fori_loop(0, i, body, (carry, acc))
    o_ref[...] = acc.astype(BF16)


def sb_attention_prompt(q, k, v):
    b, t, _ = q.shape
    blk = pl.BlockSpec((None, QBLK, SB_HEAD_DIM), lambda bi, h, i: (bi, i, h))
    full = pl.BlockSpec((None, t, SB_HEAD_DIM), lambda bi, h, i: (bi, 0, h))
    return pl.pallas_call(
        _sb_prompt_kernel,
        grid=(b, SB_HEADS, t // QBLK),
        in_specs=[blk, full, full],
        out_specs=blk,
        out_shape=jax.ShapeDtypeStruct(q.shape, BF16),
        compiler_params=_cparams(("parallel", "parallel", "arbitrary"), VMEM_LIMIT),
        name="sb_attention_prompt",
    )(q, k, v)


S5_TILE_U = 128
S5_TILE_S = S5_TILE_U // S5_GROUP * S5_STATE
S5_NTILE = D_MODEL // S5_TILE_U
S5_TILES_PER_HALF = S5_NTILE // 2
S5_LANE_TILES = S5_HALF // LANES


def _s5_kernel(x_ref, g_ref, d_ref, bx_ref, ct_ref, are_ref, aim_ref, h0re_ref, h0im_ref,
               y_ref, hre_out, him_out, xre, xim, hre, him, *, nb, tc, t_last):
    c = pl.program_id(0)
    rows_half = (nb // 2) * tc

    @pl.when(c == 0)
    def _():
        hre[...] = h0re_ref[...]
        him[...] = h0im_ref[...]

    u = _rms(x_ref[...].reshape(rows_half, D_MODEL), g_ref[...])
    u_hi, u_lo = _split_bf16(u)
    lanes_per_tile = S5_TILE_S // LANES
    for tile in range(S5_NTILE):
        half, ctile = divmod(tile, S5_TILES_PER_HALF)
        rows = slice(half * rows_half, (half + 1) * rows_half)
        ucol = slice(tile * S5_TILE_U, (tile + 1) * S5_TILE_U)
        a_hi, a_lo = u_hi[:, ucol], u_lo[:, ucol]
        re = _dot(a_hi, bx_ref[0, tile]) + _dot(a_lo, bx_ref[0, tile]) + _dot(a_hi, bx_ref[1, tile])
        im = _dot(a_hi, bx_ref[2, tile]) + _dot(a_lo, bx_ref[2, tile]) + _dot(a_hi, bx_ref[3, tile])
        for k in range(lanes_per_tile):
            lt = ctile * lanes_per_tile + k
            xre[lt, rows, :] = re[:, k * LANES:(k + 1) * LANES]
            xim[lt, rows, :] = im[:, k * LANES:(k + 1) * LANES]

    def step(t, carry):
        idx = pl.ds(t, nb, stride=tc)
        new = []
        for lt in range(S5_LANE_TILES):
            hr, hi = carry[lt]
            ar = are_ref[lt]
            ai = aim_ref[lt]
            nr = ar * hr - ai * hi + xre[lt, idx, :]
            ni = ar * hi + ai * hr + xim[lt, idx, :]
            xre[lt, idx, :] = nr
            xim[lt, idx, :] = ni
            new.append((nr, ni))
        return tuple(new)

    h_end = lax.fori_loop(0, tc, step, tuple((hre[lt], him[lt]) for lt in range(S5_LANE_TILES)))
    for lt in range(S5_LANE_TILES):
        hre[lt] = h_end[lt][0]
        him[lt] = h_end[lt][1]

    ys = []
    for tile in range(S5_NTILE):
        half, ctile = divmod(tile, S5_TILES_PER_HALF)
        rows = slice(half * rows_half, (half + 1) * rows_half)
        lts = range(ctile * lanes_per_tile, (ctile + 1) * lanes_per_tile)
        h_re = jnp.concatenate([xre[lt, rows, :] for lt in lts], axis=1).astype(BF16)
        h_im = jnp.concatenate([xim[lt, rows, :] for lt in lts], axis=1).astype(BF16)
        ys.append(_dot(h_re, ct_ref[0, tile]) + _dot(h_im, ct_ref[1, tile]))
    y = jnp.concatenate(ys, axis=1) + d_ref[...] * u
    y_ref[...] = y.reshape(y_ref.shape)

    @pl.when(c == t_last // tc)
    def _():
        idx = pl.ds(t_last % tc, nb, stride=tc)
        for lt in range(S5_LANE_TILES):
            hre_out[lt] = xre[lt, idx, :]
            him_out[lt] = xim[lt, idx, :]


def _s5_discretize(log_dt, a_re, a_im, b_re, b_im, c_re, c_im):
    dt = jnp.exp(log_dt)[:, None]
    decay = jnp.exp(dt * a_re)
    ang = dt * a_im
    ab_re = decay * jnp.cos(ang)
    ab_im = decay * jnp.sin(ang)
    den = a_re * a_re + a_im * a_im
    nr = ab_re - 1.0
    cf_re = (nr * a_re + ab_im * a_im) / den
    cf_im = (ab_im * a_re - nr * a_im) / den
    bx_re = cf_re[:, :, None] * b_re - cf_im[:, :, None] * b_im
    bx_im = cf_re[:, :, None] * b_im + cf_im[:, :, None] * b_re
    gpt = S5_TILE_U // S5_GROUP
    eye = jnp.eye(gpt, dtype=F32)

    def b_tiles(b):
        bt = b.reshape(S5_NTILE, gpt, S5_STATE, S5_GROUP)
        return jnp.einsum('tgpc,gh->tgchp', bt, eye).reshape(S5_NTILE, S5_TILE_U, S5_TILE_S)

    def c_tiles(cm):
        ctl = cm.reshape(S5_NTILE, gpt, S5_GROUP, S5_STATE)
        return jnp.einsum('tgcp,gh->tgphc', ctl, eye).reshape(S5_NTILE, S5_TILE_S, S5_TILE_U)

    bre_hi, bre_lo = _split_bf16(b_tiles(bx_re))
    bim_hi, bim_lo = _split_bf16(b_tiles(bx_im))
    bx = jnp.stack([bre_hi, bre_lo, bim_hi, bim_lo])
    ct = jnp.stack([c_tiles(c_re), -c_tiles(c_im)]).astype(BF16)
    return ab_re.reshape(-1), ab_im.reshape(-1), bx, ct


def _s5_rows(vec, n_seq):
    v = vec.reshape(2, 1, S5_LANE_TILES, LANES)
    return jnp.broadcast_to(v, (2, n_seq, S5_LANE_TILES, LANES)).transpose(2, 0, 1, 3).reshape(
        S5_LANE_TILES, 2 * n_seq, LANES)


def _s5_state_to_rows(h):
    n = h.shape[0]
    return h.reshape(n, 2, S5_LANE_TILES, LANES).transpose(2, 1, 0, 3).reshape(S5_LANE_TILES, 2 * n, LANES)


def _s5_rows_to_state(r):
    n = r.shape[1] // 2
    return r.reshape(S5_LANE_TILES, 2, n, LANES).transpose(2, 1, 0, 3).reshape(n, S5_GROUPS, S5_STATE)


def s5_scan(x3, g, d_skip, disc, h0_re, h0_im, n_seq, tc, t_last):
    ab_re, ab_im, bx, ct = disc
    gdim, t_total, _ = x3.shape
    rchunk = n_seq * tc // gdim
    nb = 2 * n_seq
    nchunks = t_total // rchunk
    st = (S5_LANE_TILES, nb, LANES)
    buf = (S5_LANE_TILES, nb * tc, LANES)
    y, hre, him = pl.pallas_call(
        functools.partial(_s5_kernel, nb=nb, tc=tc, t_last=t_last),
        grid=(nchunks,),
        in_specs=[pl.BlockSpec((gdim, rchunk, D_MODEL), lambda c: (0, c, 0)),
                  _const_spec((1, D_MODEL)), _const_spec((1, D_MODEL)),
                  _const_spec(bx.shape), _const_spec(ct.shape),
                  _const_spec(st), _const_spec(st), _const_spec(st), _const_spec(st)],
        out_specs=[pl.BlockSpec((gdim, rchunk, D_MODEL), lambda c: (0, c, 0)),
                   pl.BlockSpec(st, lambda c: (0, 0, 0)), pl.BlockSpec(st, lambda c: (0, 0, 0))],
        out_shape=[jax.ShapeDtypeStruct(x3.shape, F32),
                   jax.ShapeDtypeStruct(st, F32), jax.ShapeDtypeStruct(st, F32)],
        scratch_shapes=[pltpu.VMEM(buf, F32), pltpu.VMEM(buf, F32),
                        pltpu.VMEM(st, F32), pltpu.VMEM(st, F32)],
        compiler_params=_cparams(("arbitrary",), VMEM_LIMIT),
        name="s5_scan",
    )(x3, g.reshape(1, D_MODEL), d_skip.reshape(1, D_MODEL), bx, ct,
      _s5_rows(ab_re, n_seq), _s5_rows(ab_im, n_seq), _s5_state_to_rows(h0_re), _s5_state_to_rows(h0_im))
    return y, _s5_rows_to_state(hre), _s5_rows_to_state(him)


def kernel(x_prompt, x_sample, cache_dsa_k, cache_dsa_v, cache_dsa_idx_k, state_ssm_re, state_ssm_im,
           cache_sb_k, cache_sb_v, page_table, meta_tokens, g_mix, g_ffn,
           dsa_w_in, dsa_q_norm, dsa_k_norm, dsa_w_out,
           s5_log_dt, s5_a_re, s5_a_im, s5_b_re, s5_b_im, s5_c_re, s5_c_im, s5_d, s5_w_out,
           sb_w_in, sb_w_out, ffn_w_gu, ffn_w_down, moe_w_router, moe_b_router, moe_w_gu, moe_w_down):
    bsz, seq, _ = x_prompt.shape
    t_p = seq + N_META
    t_pad = -(-t_p // QBLK) * QBLK
    meta = jnp.broadcast_to(meta_tokens[None], (bsz, N_META, D_MODEL))
    xp = jnp.concatenate([meta, x_prompt, jnp.zeros((bsz, t_pad - t_p, D_MODEL), F32)], axis=1)
    xp = xp.reshape(bsz * t_pad, D_MODEL)
    nb, n_new, _ = x_sample.shape
    xs = x_sample.reshape(nb * n_new, D_MODEL)
    past_len = page_table.shape[1] * PAGE
    topk_p = min(TOPK_MAX, seq // 4)
    topk_s = min(TOPK_MAX, (past_len + n_new) // 4)
    tm_p = PROMPT_TILE
    tm_s = nb * n_new

    p3 = lambda a: a.reshape(bsz, t_pad, a.shape[-1])
    out_p = lambda a, *hd: a.reshape((bsz, t_pad) + hd)[:, :t_p]
    out_s = lambda a, *hd: a.reshape((nb, n_new) + hd)

    outs = {name: [] for name in ("dsa_k_p", "dsa_k_s", "dsa_v_p", "dsa_v_s", "dsa_i_p", "dsa_i_s",
                                  "ssm_re_p", "ssm_re_s", "ssm_im_p", "ssm_im_s",
                                  "sb_k_p", "sb_k_s", "sb_v_p", "sb_v_s")}
    ia = ib = ic = i_dense = i_moe = 0
    depth = g_mix.shape[0]
    for layer in range(depth):
        kind = layer % 3
        if kind == 0:
            w_in = jnp.concatenate(
                [dsa_w_in[ia], jnp.zeros((D_MODEL, DSA_IN_PAD - dsa_w_in.shape[2]), F32)], axis=1).astype(BF16)
            w_out = dsa_w_out[ia].astype(BF16)
            q, k, v, qi, ki, wi = dsa_project(xp, g_mix[layer], w_in, dsa_q_norm[ia], dsa_k_norm[ia], tm_p)
            o = dsa_attention_prompt(p3(q), p3(qi), p3(wi), p3(k), p3(v), p3(ki), topk_p)
            xp = out_project(xp, o.reshape(bsz * t_pad, D_MODEL), w_out, tm_p)
            qs, ks, vs, qis, kis, wis = dsa_project(xs, g_mix[layer], w_in, dsa_q_norm[ia], dsa_k_norm[ia], tm_s)
            o_s = dsa_sample(page_table, qs, qis, wis, ks, vs, kis,
                             cache_dsa_k, cache_dsa_v, cache_dsa_idx_k, ia, topk_s)
            xs = out_project(xs, o_s.astype(BF16), w_out, tm_s)
            outs["dsa_k_p"].append(out_p(k, DSA_KV_HEADS, HEAD_DIM))
            outs["dsa_v_p"].append(out_p(v, DSA_KV_HEADS, HEAD_DIM))
            outs["dsa_i_p"].append(out_p(ki, IDX_DIM))
            outs["dsa_k_s"].append(out_s(ks, DSA_KV_HEADS, HEAD_DIM))
            outs["dsa_v_s"].append(out_s(vs, DSA_KV_HEADS, HEAD_DIM))
            outs["dsa_i_s"].append(out_s(kis, IDX_DIM))
            ia += 1
        elif kind == 1:
            disc = _s5_discretize(s5_log_dt[ib], s5_a_re[ib], s5_a_im[ib], s5_b_re[ib], s5_b_im[ib],
                                  s5_c_re[ib], s5_c_im[ib])
            w_out = s5_w_out[ib].astype(BF16)
            zeros = jnp.zeros((bsz, S5_GROUPS, S5_STATE), F32)
            yp, hre_p, him_p = s5_scan(p3(xp), g_mix[layer], s5_d[ib], disc, zeros, zeros, bsz, QBLK, t_p - 1)
            xp = s5_glu(xp, yp.reshape(bsz * t_pad, D_MODEL), w_out, tm_p)
            ys, hre_s, him_s = s5_scan(xs.reshape(1, nb * n_new, D_MODEL), g_mix[layer], s5_d[ib], disc,
                                       state_ssm_re[ib], state_ssm_im[ib], nb, n_new, n_new - 1)
            xs = s5_glu(xs, ys.reshape(nb * n_new, D_MODEL), w_out, tm_s)
            outs["ssm_re_p"].append(hre_p)
            outs["ssm_im_p"].append(him_p)
            outs["ssm_re_s"].append(hre_s)
            outs["ssm_im_s"].append(him_s)
            ib += 1
        else:
            w_in = sb_w_in[ic].astype(BF16)
            w_out = sb_w_out[ic].astype(BF16)
            q, k, v = sb_project(xp, g_mix[layer], w_in, tm_p)
            o = sb_attention_prompt(p3(q), p3(k), p3(v))
            xp = out_project(xp, o.reshape(bsz * t_pad, D_MODEL), w_out, tm_p)
            qs, ks, vs = sb_project(xs, g_mix[layer], w_in, tm_s)
            o_s = sb_sample(page_table, qs, ks, vs, cache_sb_k, cache_sb_v, ic)
            xs = out_project(xs, o_s.astype(BF16), w_out, tm_s)
            outs["sb_k_p"].append(out_p(k, SB_HEADS, SB_HEAD_DIM))
            outs["sb_v_p"].append(out_p(v, SB_HEADS, SB_HEAD_DIM))
            outs["sb_k_s"].append(out_s(ks, SB_HEADS, SB_HEAD_DIM))
            outs["sb_v_s"].append(out_s(vs, SB_HEADS, SB_HEAD_DIM))
            ic += 1
        if layer % 2 == 0:
            wgu = ffn_w_gu[i_dense].astype(BF16)
            wd = ffn_w_down[i_dense].astype(BF16)
            xp = ffn(xp, g_ffn[layer], wgu, wd, tm_p)
            xs = ffn(xs, g_ffn[layer], wgu, wd, tm_s)
            i_dense += 1
        else:
            wr = jnp.concatenate([moe_w_router[i_moe], jnp.zeros((D_MODEL, LANES - N_EXPERTS), F32)], axis=1)
            br = jnp.concatenate([moe_b_router[i_moe], jnp.full((LANES - N_EXPERTS,), NEG_BIG, F32)])
            br = br.reshape(1, LANES)
            wgu = moe_w_gu[i_moe].astype(BF16)
            wd = moe_w_down[i_moe].astype(BF16)
            xp = moe(xp, g_ffn[layer], wr, br, wgu, wd, tm_p)
            xs = moe(xs, g_ffn[layer], wr, br, wgu, wd, tm_s)
            i_moe += 1

    y_prompt = xp.reshape(bsz, t_pad, D_MODEL)[:, N_META:t_p]
    y_sample = xs.reshape(nb, n_new, D_MODEL)
    stack = lambda name: jnp.stack(outs[name])
    return (y_prompt, y_sample,
            stack("dsa_k_p"), stack("dsa_k_s"), stack("dsa_v_p"), stack("dsa_v_s"),
            stack("dsa_i_p"), stack("dsa_i_s"),
            stack("ssm_re_p"), stack("ssm_re_s"), stack("ssm_im_p"), stack("ssm_im_s"),
            stack("sb_k_p"), stack("sb_k_s"), stack("sb_v_p"), stack("sb_v_s"))
```

```python
import functools
import math

import jax
import jax.numpy as jnp
from jax import lax
from jax.experimental import pallas as pl
from jax.experimental.pallas import tpu as pltpu

F32 = jnp.float32
BF16 = jnp.bfloat16
I32 = jnp.int32

D_MODEL = 1024
N_META = 16
EPS = 1e-6
DSA_HEADS = 8
DSA_KV_HEADS = 2
DSA_GROUP = DSA_HEADS // DSA_KV_HEADS
HEAD_DIM = 128
IDX_HEADS = 8
IDX_DIM = 64
TOPK_MAX = 256
S5_GROUP = 16
S5_GROUPS = 64
S5_STATE = 64
S5_NSTATE = S5_GROUPS * S5_STATE
S5_HALF = S5_NSTATE // 2
SB_HEADS = 8
SB_HEAD_DIM = 128
FF_DIM = 11 * D_MODEL // 4
N_EXPERTS = 8
PAGE = 128

LANES = 128
SUBLANES = 8
VMEM_LIMIT = 56 * 1024 * 1024

QBLK = 128
KEY_SUB = 3
KEY_CHUNK = KEY_SUB * QBLK
PROMPT_TILE = 512
DSA_IN_DIM = (DSA_HEADS + 2 * DSA_KV_HEADS) * HEAD_DIM + IDX_HEADS * IDX_DIM + IDX_DIM + IDX_HEADS
DSA_IN_PAD = -(-DSA_IN_DIM // LANES) * LANES
INT_MIN = -(2 ** 31)
NEG_BIG = -1e30


def _cparams(sem, vmem=None):
    return pltpu.CompilerParams(dimension_semantics=sem, vmem_limit_bytes=vmem)


def _const_spec(shape):
    nd = len(shape)
    return pl.BlockSpec(shape, lambda *_: (0,) * nd, pipeline_mode=pl.Buffered(1))


def _rms(x, g):
    return x * lax.rsqrt(jnp.mean(x * x, axis=-1, keepdims=True) + EPS) * g


def _dot(a, b):
    return jnp.dot(a, b, preferred_element_type=F32)


def _dot_nt(a, b):
    return lax.dot_general(a, b, (((1,), (1,)), ((), ())), preferred_element_type=F32)


def _split_bf16(x):
    hi = x.astype(BF16)
    lo = (x - hi.astype(F32)).astype(BF16)
    return hi, lo


def _lane_tiles(x):
    return [x[:, u * LANES:(u + 1) * LANES] for u in range(x.shape[-1] // LANES)]


def _dsa_proj_kernel(x_ref, g_ref, w_ref, qg_ref, kg_ref,
                     q_ref, k_ref, v_ref, qi_ref, ki_ref, wi_ref):
    h = _rms(x_ref[...], g_ref[...]).astype(BF16)
    nq = DSA_HEADS * HEAD_DIM
    nkv = DSA_KV_HEADS * HEAD_DIM
    q = _dot(h, w_ref[:, 0:nq])
    scale = 1.0 / math.sqrt(HEAD_DIM)
    for hd in range(DSA_HEADS):
        sl = slice(hd * HEAD_DIM, (hd + 1) * HEAD_DIM)
        q_ref[:, sl] = (_rms(q[:, sl], qg_ref[...]) * scale).astype(BF16)
    k = _dot(h, w_ref[:, nq:nq + nkv])
    for hd in range(DSA_KV_HEADS):
        sl = slice(hd * HEAD_DIM, (hd + 1) * HEAD_DIM)
        k_ref[:, sl] = _rms(k[:, sl], kg_ref[...])
    v_ref[...] = _dot(h, w_ref[:, nq + nkv:nq + 2 * nkv])
    o = nq + 2 * nkv
    qi_ref[...] = _dot(h, w_ref[:, o:o + IDX_HEADS * IDX_DIM]).astype(BF16)
    o += IDX_HEADS * IDX_DIM
    tail = _dot(h, w_ref[:, o:o + LANES])
    ki_ref[...] = tail[:, 0:IDX_DIM]
    wi_ref[...] = tail[:, IDX_DIM:IDX_DIM + IDX_HEADS]


def dsa_project(x, g, w_pad, qg, kg, tm):
    m = x.shape[0]
    assert m % tm == 0
    nq = DSA_HEADS * HEAD_DIM
    nkv = DSA_KV_HEADS * HEAD_DIM
    row = lambda n: pl.BlockSpec((tm, n), lambda i: (i, 0))
    return pl.pallas_call(
        _dsa_proj_kernel,
        grid=(m // tm,),
        in_specs=[row(D_MODEL), _const_spec((1, D_MODEL)), _const_spec(w_pad.shape),
                  _const_spec((1, HEAD_DIM)), _const_spec((1, HEAD_DIM))],
        out_specs=[row(nq), row(nkv), row(nkv), row(IDX_HEADS * IDX_DIM), row(IDX_DIM), row(IDX_HEADS)],
        out_shape=[jax.ShapeDtypeStruct((m, nq), BF16),
                   jax.ShapeDtypeStruct((m, nkv), F32),
                   jax.ShapeDtypeStruct((m, nkv), F32),
                   jax.ShapeDtypeStruct((m, IDX_HEADS * IDX_DIM), BF16),
                   jax.ShapeDtypeStruct((m, IDX_DIM), F32),
                   jax.ShapeDtypeStruct((m, IDX_HEADS), F32)],
        compiler_params=_cparams(("parallel",), VMEM_LIMIT),
        name="dsa_project",
    )(x, g.reshape(1, D_MODEL), w_pad, qg.reshape(1, HEAD_DIM), kg.reshape(1, HEAD_DIM))


def _sb_proj_kernel(x_ref, g_ref, w_ref, q_ref, k_ref, v_ref, kb_ref, vb_ref):
    h = _rms(x_ref[...], g_ref[...]).astype(BF16)
    scale = 1.0 / math.sqrt(SB_HEAD_DIM)
    q_ref[...] = (_dot(h, w_ref[:, 0:D_MODEL]) * scale).astype(BF16)
    k = _dot(h, w_ref[:, D_MODEL:2 * D_MODEL])
    k_ref[...] = k
    kb_ref[...] = k.astype(BF16)
    v = _dot(h, w_ref[:, 2 * D_MODEL:3 * D_MODEL])
    v_ref[...] = v
    vb_ref[...] = v.astype(BF16)


def sb_project(x, g, w, tm):
    m = x.shape[0]
    assert m % tm == 0
    row = lambda n: pl.BlockSpec((tm, n), lambda i: (i, 0))
    return pl.pallas_call(
        _sb_proj_kernel,
        grid=(m // tm,),
        in_specs=[row(D_MODEL), _const_spec((1, D_MODEL)), _const_spec(w.shape)],
        out_specs=[row(D_MODEL)] * 5,
        out_shape=[jax.ShapeDtypeStruct((m, D_MODEL), BF16),
                   jax.ShapeDtypeStruct((m, D_MODEL), F32),
                   jax.ShapeDtypeStruct((m, D_MODEL), F32),
                   jax.ShapeDtypeStruct((m, D_MODEL), BF16),
                   jax.ShapeDtypeStruct((m, D_MODEL), BF16)],
        compiler_params=_cparams(("parallel",), VMEM_LIMIT),
        name="sb_project",
    )(x, g.reshape(1, D_MODEL), w)


def _out_proj_kernel(x_ref, o_ref, w_ref, y_ref):
    y_ref[...] = x_ref[...] + _dot(o_ref[...], w_ref[...])


def out_project(x, o, w, tm):
    m = x.shape[0]
    assert m % tm == 0
    row = pl.BlockSpec((tm, D_MODEL), lambda i: (i, 0))
    return pl.pallas_call(
        _out_proj_kernel,
        grid=(m // tm,),
        in_specs=[row, row, _const_spec(w.shape)],
        out_specs=row,
        out_shape=jax.ShapeDtypeStruct((m, D_MODEL), F32),
        compiler_params=_cparams(("parallel",), VMEM_LIMIT),
        name="out_project",
    )(x, o, w)


def _gelu_tanh(y):
    c = math.sqrt(2.0 / math.pi)
    return 0.5 * y * (1.0 + jnp.tanh(c * (y + 0.044715 * (y * y * y))))


def _s5_glu_kernel(x_ref, y_ref, w_ref, o_ref):
    z = _gelu_tanh(y_ref[...]).astype(BF16)
    a = _dot(z, w_ref[:, 0:D_MODEL])
    g = _dot(z, w_ref[:, D_MODEL:2 * D_MODEL])
    o_ref[...] = x_ref[...] + a * jax.nn.sigmoid(g)


def s5_glu(x, y, w, tm):
    m = x.shape[0]
    assert m % tm == 0
    row = pl.BlockSpec((tm, D_MODEL), lambda i: (i, 0))
    return pl.pallas_call(
        _s5_glu_kernel,
        grid=(m // tm,),
        in_specs=[row, row, _const_spec(w.shape)],
        out_specs=row,
        out_shape=jax.ShapeDtypeStruct((m, D_MODEL), F32),
        compiler_params=_cparams(("parallel",), VMEM_LIMIT),
        name="s5_glu",
    )(x, y, w)


FF_CHUNK = 256


def _swiglu_acc(h, wgu_ref, wd_ref, acc):
    for c in range(FF_DIM // FF_CHUNK):
        lo = c * FF_CHUNK
        g = _dot(h, wgu_ref[:, lo:lo + FF_CHUNK])
        u = _dot(h, wgu_ref[:, FF_DIM + lo:FF_DIM + lo + FF_CHUNK])
        a = (g * jax.nn.sigmoid(g) * u).astype(BF16)
        acc = acc + _dot(a, wd_ref[lo:lo + FF_CHUNK, :])
    return acc


def _ffn_kernel(x_ref, g_ref, wgu_ref, wd_ref, o_ref):
    x = x_ref[...]
    h = _rms(x, g_ref[...]).astype(BF16)
    o_ref[...] = _swiglu_acc(h, wgu_ref, wd_ref, x)


def ffn(x, g, wgu, wd, tm):
    m = x.shape[0]
    assert m % tm == 0
    row = pl.BlockSpec((tm, D_MODEL), lambda i: (i, 0))
    return pl.pallas_call(
        _ffn_kernel,
        grid=(m // tm,),
        in_specs=[row, _const_spec((1, D_MODEL)), _const_spec(wgu.shape), _const_spec(wd.shape)],
        out_specs=row,
        out_shape=jax.ShapeDtypeStruct((m, D_MODEL), F32),
        compiler_params=_cparams(("parallel",), VMEM_LIMIT),
        name="ffn",
    )(x, g.reshape(1, D_MODEL), wgu, wd)


def _moe_kernel(x_ref, g_ref, wr_ref, br_ref, wgu_ref, wd_ref, o_ref, h_scr, comb_scr):
    e = pl.program_id(1)

    @pl.when(e == 0)
    def _():
        x = x_ref[...]
        hf = _rms(x, g_ref[...])
        h_scr[...] = hf.astype(BF16)
        h_hi, h_lo = _split_bf16(hf)
        w_hi, w_lo = _split_bf16(wr_ref[...])
        logits = _dot(h_hi, w_hi) + _dot(h_lo, w_hi) + _dot(h_hi, w_lo) + br_ref[...]
        lane = lax.broadcasted_iota(I32, logits.shape, 1)
        m1 = jnp.max(logits, axis=-1, keepdims=True)
        i1 = jnp.min(jnp.where(logits == m1, lane, LANES), axis=-1, keepdims=True)
        rest = jnp.where(lane == i1, NEG_BIG, logits)
        m2 = jnp.max(rest, axis=-1, keepdims=True)
        i2 = jnp.min(jnp.where(rest == m2, lane, LANES), axis=-1, keepdims=True)
        e2 = jnp.exp(m2 - m1)
        g1 = 1.0 / (1.0 + e2)
        g2 = e2 / (1.0 + e2)
        comb_scr[...] = jnp.where(lane == i1, g1, 0.0) + jnp.where(lane == i2, g2, 0.0)
        o_ref[...] = x

    lane = lax.broadcasted_iota(I32, comb_scr.shape, 1)
    gate = jnp.sum(jnp.where(lane == e, comb_scr[...], 0.0), axis=-1, keepdims=True)
    y = _swiglu_acc(h_scr[...], wgu_ref.at[0], wd_ref.at[0], jnp.zeros(o_ref.shape, F32))
    o_ref[...] = o_ref[...] + gate * y


def moe(x, g, w_router_pad, b_router_pad, wgu, wd, tm):
    m = x.shape[0]
    assert m % tm == 0
    row = pl.BlockSpec((tm, D_MODEL), lambda i, e: (i, 0))
    cst = lambda shape: pl.BlockSpec(shape, lambda i, e: (0,) * len(shape), pipeline_mode=pl.Buffered(1))
    return pl.pallas_call(
        _moe_kernel,
        grid=(m // tm, N_EXPERTS),
        in_specs=[row, cst((1, D_MODEL)), cst((D_MODEL, LANES)), cst((1, LANES)),
                  pl.BlockSpec((1, D_MODEL, 2 * FF_DIM), lambda i, e: (e, 0, 0)),
                  pl.BlockSpec((1, FF_DIM, D_MODEL), lambda i, e: (e, 0, 0))],
        out_specs=row,
        out_shape=jax.ShapeDtypeStruct((m, D_MODEL), F32),
        scratch_shapes=[pltpu.VMEM((tm, D_MODEL), BF16), pltpu.VMEM((tm, LANES), F32)],
        compiler_params=_cparams(("parallel", "arbitrary"), VMEM_LIMIT),
        name="moe",
    )(x, g.reshape(1, D_MODEL), w_router_pad, b_router_pad, wgu, wd)


def _sortable_key(s):
    bits = pltpu.bitcast(s, I32)
    return bits ^ ((bits >> 31) & 0x7FFFFFFF)


def _kth_largest_key(count_ge, topk, shape):
    zero = jnp.zeros(shape, I32)
    t0 = jnp.where(count_ge(zero) >= topk, zero, jnp.full(shape, INT_MIN, I32))

    def body(k, t):
        cand = t | jnp.left_shift(jnp.int32(1), 30 - k)
        return jnp.where(count_ge(cand) >= topk, cand, t)

    t = lax.fori_loop(0, 31, body, t0)
    return jnp.maximum(t, INT_MIN + 1)


def _indexer_scores(lg, wib_ref, rows):
    tiles = []
    for u in range(lg.shape[1] // LANES):
        cols = slice(u * LANES, (u + 1) * LANES)
        su = wib_ref[0] * jnp.maximum(lg[0:rows, cols], 0.0)
        for h in range(1, IDX_HEADS):
            su = su + wib_ref[h] * jnp.maximum(lg[h * rows:(h + 1) * rows, cols], 0.0)
        tiles.append(su)
    return tiles


def _count_matrix():
    r = lax.broadcasted_iota(I32, (QBLK, 2 * QBLK), 0)
    c = lax.broadcasted_iota(I32, (QBLK, 2 * QBLK), 1)
    return jnp.where((r < c) | (c >= QBLK), 1.0, 0.0).astype(BF16)


def _select_chunk(key, t, need, carry, count_mat):
    eq = key == t
    cnt = _dot(jnp.where(eq, 1.0, 0.0).astype(BF16), count_mat)
    before = cnt[:, :QBLK] + carry
    sel = (key > t) | (eq & (before < need))
    return sel, carry + cnt[:, QBLK:]


def _softmax_update(groups, sels, m_ref, l_ref, acc_ref, rows):
    old = {hd: (m_ref[hd], l_ref[hd], acc_ref[hd]) for _, _, heads in groups for hd in heads}
    new = {}
    for s, v, heads in groups:
        ps = []
        m_news = []
        for j, hd in enumerate(heads):
            tiles = _lane_tiles(s[j * rows:(j + 1) * rows])
            mx = jnp.where(sels[0], tiles[0], NEG_BIG)
            for sel, tile in zip(sels[1:], tiles[1:]):
                mx = jnp.maximum(mx, jnp.where(sel, tile, NEG_BIG))
            m_new = jnp.maximum(old[hd][0], jnp.max(mx, axis=-1, keepdims=True))
            p = [jnp.where(sel, jnp.exp(tile - m_new), 0.0) for sel, tile in zip(sels, tiles)]
            ps.append(jnp.concatenate(p, axis=1).astype(BF16) if len(p) > 1 else p[0].astype(BF16))
            m_news.append(m_new)
        pv = _dot(jnp.concatenate(ps, axis=0), jnp.concatenate([v, jnp.ones(v.shape, BF16)], axis=1))
        for j, hd in enumerate(heads):
            m_old, l_old, acc_old = old[hd]
            blk = pv[j * rows:(j + 1) * rows]
            alpha = jnp.exp(m_old - m_news[j])
            new[hd] = (m_news[j], alpha * l_old + blk[:, LANES:], alpha * acc_old + blk[:, :LANES])
    for hd, (m_new, l_new, acc_new) in new.items():
        m_ref[hd] = m_new
        l_ref[hd] = l_new
        acc_ref[hd] = acc_new


def _dsa_prompt_kernel(q_ref, qi_ref, wi_ref, k_ref, v_ref, ki_ref, o_ref,
                       skey_ref, wib_ref, t_ref, need_ref, m_ref, l_ref, acc_ref, *, topk):
    i = pl.program_id(1)
    nsup = i // KEY_SUB + 1
    shape = (QBLK, QBLK)
    row_pos = i * QBLK + lax.broadcasted_iota(I32, shape, 0)
    col_iota = lax.broadcasted_iota(I32, shape, 1)

    qi = qi_ref[...]
    qih = jnp.concatenate([qi[:, h * IDX_DIM:(h + 1) * IDX_DIM] for h in range(IDX_HEADS)], axis=0)
    wi = wi_ref[...]
    for h in range(IDX_HEADS):
        wib_ref[h] = jnp.broadcast_to(wi[:, h:h + 1], shape)

    def score_super(sc, _):
        start = pl.multiple_of(sc * KEY_CHUNK, KEY_CHUNK)
        kic = ki_ref[pl.ds(start, KEY_CHUNK), :].astype(BF16)
        for u, su in enumerate(_indexer_scores(_dot_nt(qih, kic), wib_ref, QBLK)):
            causal = start + u * QBLK + col_iota <= row_pos
            skey_ref[sc * KEY_SUB + u] = jnp.where(causal, _sortable_key(su), INT_MIN)
        return 0

    lax.fori_loop(0, nsup, score_super, 0)

    def count_ge(cand):
        def body(sc, acc):
            for u in range(KEY_SUB):
                acc = acc + jnp.where(skey_ref[sc * KEY_SUB + u] >= cand, 1.0, 0.0)
            return acc
        acc = lax.fori_loop(0, nsup, body, jnp.zeros(shape, F32))
        return jnp.sum(acc, axis=-1, keepdims=True)

    @pl.when((i + 1) * QBLK <= topk)
    def _():
        t_ref[...] = jnp.full(shape, INT_MIN + 1, I32)
        need_ref[...] = jnp.zeros(shape, F32)

    @pl.when((i + 1) * QBLK > topk)
    def _():
        t = _kth_largest_key(count_ge, float(topk), shape)
        t_ref[...] = t
        need_ref[...] = jnp.broadcast_to(float(topk) - count_ge(t + 1), shape)

    q = q_ref[...]
    qn = [jnp.concatenate([q[:, (n * DSA_GROUP + g) * HEAD_DIM:(n * DSA_GROUP + g + 1) * HEAD_DIM]
                           for g in range(DSA_GROUP)], axis=0) for n in range(DSA_KV_HEADS)]
    m_ref[...] = jnp.full(m_ref.shape, NEG_BIG, F32)
    l_ref[...] = jnp.zeros(l_ref.shape, F32)
    acc_ref[...] = jnp.zeros(acc_ref.shape, F32)
    count_mat = _count_matrix()
    t = t_ref[...]
    need = need_ref[...]

    def attn_super(sc, carry):
        start = pl.multiple_of(sc * KEY_CHUNK, KEY_CHUNK)
        sels = []
        for u in range(KEY_SUB):
            sel, carry = _select_chunk(skey_ref[sc * KEY_SUB + u], t, need, carry, count_mat)
            sels.append(sel)
        kc = k_ref[pl.ds(start, KEY_CHUNK), :].astype(BF16)
        vc = v_ref[pl.ds(start, KEY_CHUNK), :].astype(BF16)
        groups = []
        for n in range(DSA_KV_HEADS):
            sl = slice(n * HEAD_DIM, (n + 1) * HEAD_DIM)
            heads = [n * DSA_GROUP + g for g in range(DSA_GROUP)]
            groups.append((_dot_nt(qn[n], kc[:, sl]), vc[:, sl], heads))
        _softmax_update(groups, sels, m_ref, l_ref, acc_ref, QBLK)
        return carry

    lax.fori_loop(0, nsup, attn_super, jnp.zeros(shape, F32))
    for h in range(DSA_HEADS):
        o_ref[:, h * HEAD_DIM:(h + 1) * HEAD_DIM] = (acc_ref[h] / l_ref[h]).astype(BF16)


def dsa_attention_prompt(q, qi, wi, k, v, ki, topk):
    b, t, _ = q.shape
    assert t % KEY_CHUNK == 0
    nq = t // QBLK
    blk = lambda n: pl.BlockSpec((None, QBLK, n), lambda bi, i: (bi, i, 0))
    full = lambda n: pl.BlockSpec((None, t, n), lambda bi, i: (bi, 0, 0))
    hshape = (DSA_HEADS, QBLK, QBLK)
    return pl.pallas_call(
        functools.partial(_dsa_prompt_kernel, topk=topk),
        grid=(b, nq),
        in_specs=[blk(q.shape[2]), blk(qi.shape[2]), blk(wi.shape[2]),
                  full(k.shape[2]), full(v.shape[2]), full(ki.shape[2])],
        out_specs=blk(q.shape[2]),
        out_shape=jax.ShapeDtypeStruct(q.shape, BF16),
        scratch_shapes=[pltpu.VMEM((nq, QBLK, QBLK), I32),
                        pltpu.VMEM((IDX_HEADS, QBLK, QBLK), F32),
                        pltpu.VMEM((QBLK, QBLK), I32),
                        pltpu.VMEM((QBLK, QBLK), F32),
                        pltpu.VMEM(hshape, F32), pltpu.VMEM(hshape, F32), pltpu.VMEM(hshape, F32)],
        compiler_params=_cparams(("parallel", "arbitrary"), VMEM_LIMIT),
        name="dsa_attention_prompt",
    )(q, qi, wi, k, v, ki)


def _suffix_matrix():
    r = lax.broadcasted_iota(I32, (QBLK, 2 * QBLK), 0)
    c = lax.broadcasted_iota(I32, (QBLK, 2 * QBLK), 1)
    return jnp.where((r > c) | (c >= QBLK), 1.0, 0.0).astype(BF16)


def _sb_weights(z, carry, suffix_mat, strict):
    rows = z.shape[0]
    lk = -(jnp.maximum(z, 0.0) + jnp.log(1.0 + jnp.exp(-jnp.abs(z))))
    lkm = lk if strict is None else jnp.where(strict, lk, 0.0)
    hi, lo = _split_bf16(lkm)
    his, los = _lane_tiles(hi), _lane_tiles(lo)
    nsub = len(his)
    sufs = [None] * nsub
    for u in reversed(range(nsub)):
        r2 = _dot(jnp.concatenate([his[u], los[u]], axis=0), suffix_mat)
        ru = r2[:rows] + r2[rows:]
        sufs[u] = ru[:, :QBLK] + carry
        carry = carry + ru[:, QBLK:]
    suffix = jnp.concatenate(sufs, axis=1) if nsub > 1 else sufs[0]
    w = jnp.exp(z + lk + suffix)
    if strict is not None:
        w = jnp.where(strict, w, 0.0)
    return carry, w


SB_HEADS_PER_STEP = 4


def _sb_prompt_kernel(q_ref, k_ref, v_ref, o_ref, carry_ref, acc_ref):
    i = pl.program_id(2)
    top = i // KEY_SUB
    q = q_ref[...]
    suffix_mat = _suffix_matrix()
    shape = (QBLK, KEY_CHUNK)
    row_pos = i * QBLK + lax.broadcasted_iota(I32, shape, 0)
    col_iota = lax.broadcasted_iota(I32, shape, 1)
    carry_ref[...] = jnp.zeros(carry_ref.shape, F32)
    acc_ref[...] = jnp.zeros(acc_ref.shape, F32)

    def run(sc, masked):
        start = pl.multiple_of(sc * KEY_CHUNK, KEY_CHUNK)
        kc = k_ref[pl.ds(start, KEY_CHUNK), :]
        vc = v_ref[pl.ds(start, KEY_CHUNK), :]
        strict = (start + col_iota < row_pos) if masked else None
        old = [(carry_ref[h], acc_ref[h]) for h in range(SB_HEADS_PER_STEP)]
        new = []
        for h in range(SB_HEADS_PER_STEP):
            sl = slice(h * SB_HEAD_DIM, (h + 1) * SB_HEAD_DIM)
            carry, w = _sb_weights(_dot_nt(q[:, sl], kc[:, sl]), old[h][0], suffix_mat, strict)
            new.append((carry, old[h][1] + _dot(w.astype(BF16), vc[:, sl])))
        for h in range(SB_HEADS_PER_STEP):
            carry_ref[h] = new[h][0]
            acc_ref[h] = new[h][1]

    run(top, True)

    def body(j, _):
        run(top - 1 - j, False)
        return 0

    lax.fori_loop(0, top, body, 0)
    for h in range(SB_HEADS_PER_STEP):
        o_ref[:, h * SB_HEAD_DIM:(h + 1) * SB_HEAD_DIM] = acc_ref[h].astype(BF16)


def sb_attention_prompt(q, k, v):
    b, t, _ = q.shape
    assert t % KEY_CHUNK == 0
    width = SB_HEADS_PER_STEP * SB_HEAD_DIM
    blk = pl.BlockSpec((None, QBLK, width), lambda bi, h, i: (bi, i, h))
    full = pl.BlockSpec((None, t, width), lambda bi, h, i: (bi, 0, h))
    hshape = (SB_HEADS_PER_STEP, QBLK, QBLK)
    return pl.pallas_call(
        _sb_prompt_kernel,
        grid=(b, SB_HEADS // SB_HEADS_PER_STEP, t // QBLK),
        in_specs=[blk, full, full],
        out_specs=blk,
        out_shape=jax.ShapeDtypeStruct(q.shape, BF16),
        scratch_shapes=[pltpu.VMEM(hshape, F32), pltpu.VMEM(hshape, F32)],
        compiler_params=_cparams(("parallel", "parallel", "arbitrary"), VMEM_LIMIT),
        name="sb_attention_prompt",
    )(q, k, v)


SELECT_PAGES = 16
ATTN_PAGES = 8
QPAD = SUBLANES


def _page_specs(n_slots, rows, width, layer, order):
    def spec(r):
        return pl.BlockSpec((None, None, rows, width),
                            lambda b, j, pt: (layer, pt[b, order(j, r)], 0, 0))
    return [spec(r) for r in range(n_slots)]


def _head_rows(page_ref, head, n_heads):
    return page_ref[pl.ds(head, PAGE, stride=n_heads), :]


def _dsa_sample_select_kernel(pt_ref, qi_ref, wi_ref, *refs, n_pages, n_new, topk):
    ki_refs = refs[:SELECT_PAGES]
    kin_ref, sel_ref, skey_ref, wib_ref, qih_ref = refs[SELECT_PAGES:]
    j = pl.program_id(1)
    shape = (QPAD, LANES)

    @pl.when(j == 0)
    def _():
        qi = qi_ref[...].astype(F32)
        qih_ref[...] = jnp.concatenate(
            [qi[:, h * IDX_DIM:(h + 1) * IDX_DIM] for h in range(IDX_HEADS)], axis=0).astype(BF16)
        wi = wi_ref[...]
        for h in range(IDX_HEADS):
            wib_ref[h] = jnp.broadcast_to(wi[:, h:h + 1], shape)

    def scores(ki):
        lg = _dot_nt(qih_ref[...], ki.astype(BF16))
        return [_sortable_key(su) for su in _indexer_scores(lg, wib_ref, QPAD)]

    for r, ku in enumerate(scores(jnp.concatenate([r[...] for r in ki_refs], axis=0))):
        skey_ref[j * SELECT_PAGES + r] = ku

    @pl.when(j == pl.num_programs(1) - 1)
    def _():
        row = lax.broadcasted_iota(I32, shape, 0)
        col = lax.broadcasted_iota(I32, shape, 1)
        visible = (col <= row) & (row < n_new)
        skey_ref[n_pages] = jnp.where(visible, scores(kin_ref[...])[0], INT_MIN)
        nchunk = n_pages + 1

        def count_ge(cand):
            def body(c, acc):
                return acc + jnp.where(skey_ref[c] >= cand, 1.0, 0.0)
            acc = lax.fori_loop(0, nchunk, body, jnp.zeros(shape, F32))
            return jnp.sum(acc, axis=-1, keepdims=True)

        t = _kth_largest_key(count_ge, float(topk), shape)
        need = jnp.broadcast_to(float(topk) - count_ge(t + 1), shape)
        count_mat = _count_matrix()

        def emit(c, carry):
            sel, carry = _select_chunk(skey_ref[c], t, need, carry, count_mat)
            sel_ref[c] = jnp.where(sel, 1.0, 0.0)
            return carry

        lax.fori_loop(0, nchunk, emit, jnp.zeros(shape, F32))


def dsa_sample_select(page_table, qi8, wi8, cache_idx_k, layer, ki_new_pad, n_new, topk):
    nb, n_pages = page_table.shape
    assert n_pages % SELECT_PAGES == 0
    per_b = lambda shape: pl.BlockSpec((None,) + shape, lambda b, j, pt: (b,) + (0,) * len(shape))
    grid_spec = pltpu.PrefetchScalarGridSpec(
        num_scalar_prefetch=1,
        grid=(nb, n_pages // SELECT_PAGES),
        in_specs=[per_b((QPAD, IDX_HEADS * IDX_DIM)), per_b((QPAD, IDX_HEADS))]
        + _page_specs(SELECT_PAGES, PAGE, IDX_DIM, layer, lambda j, r: j * SELECT_PAGES + r)
        + [per_b((PAGE, IDX_DIM))],
        out_specs=per_b((n_pages + 1, QPAD, LANES)),
        scratch_shapes=[pltpu.VMEM((n_pages + 1, QPAD, LANES), I32),
                        pltpu.VMEM((IDX_HEADS, QPAD, LANES), F32),
                        pltpu.VMEM((IDX_HEADS * QPAD, IDX_DIM), BF16)],
    )
    return pl.pallas_call(
        functools.partial(_dsa_sample_select_kernel, n_pages=n_pages, n_new=n_new, topk=topk),
        grid_spec=grid_spec,
        out_shape=jax.ShapeDtypeStruct((nb, n_pages + 1, QPAD, LANES), F32),
        compiler_params=_cparams(("parallel", "arbitrary"), VMEM_LIMIT),
        name="dsa_sample_select",
    )(page_table, qi8, wi8, *([cache_idx_k] * SELECT_PAGES), ki_new_pad)


def _dsa_sample_attn_kernel(pt_ref, q_ref, sel_ref, *refs, n_pages):
    k_refs = refs[:ATTN_PAGES]
    v_refs = refs[ATTN_PAGES:2 * ATTN_PAGES]
    kn_ref, vn_ref, o_ref, qn_ref, m_ref, l_ref, acc_ref = refs[2 * ATTN_PAGES:]
    j = pl.program_id(1)

    @pl.when(j == 0)
    def _():
        q = q_ref[...].astype(F32)
        for n in range(DSA_KV_HEADS):
            qn_ref[n] = jnp.concatenate(
                [q[:, (n * DSA_GROUP + g) * HEAD_DIM:(n * DSA_GROUP + g + 1) * HEAD_DIM]
                 for g in range(DSA_GROUP)], axis=0).astype(BF16)
        m_ref[...] = jnp.full(m_ref.shape, NEG_BIG, F32)
        l_ref[...] = jnp.zeros(l_ref.shape, F32)
        acc_ref[...] = jnp.zeros(acc_ref.shape, F32)

    def attend(k_of, v_of, sels):
        groups = []
        for n in range(DSA_KV_HEADS):
            heads = [n * DSA_GROUP + g for g in range(DSA_GROUP)]
            groups.append((_dot_nt(qn_ref[n], k_of(n).astype(BF16)), v_of(n).astype(BF16), heads))
        _softmax_update(groups, sels, m_ref, l_ref, acc_ref, QPAD)

    gather = lambda page_refs: (lambda n: jnp.concatenate(
        [_head_rows(r, n, DSA_KV_HEADS) for r in page_refs], axis=0))
    attend(gather(k_refs), gather(v_refs), [sel_ref[j * ATTN_PAGES + r] > 0.5 for r in range(ATTN_PAGES)])

    @pl.when(j == pl.num_programs(1) - 1)
    def _():
        head = lambda ref: (lambda n: ref[:, n * HEAD_DIM:(n + 1) * HEAD_DIM])
        attend(head(kn_ref), head(vn_ref), [sel_ref[n_pages] > 0.5])
        for h in range(DSA_HEADS):
            o_ref[:, h * HEAD_DIM:(h + 1) * HEAD_DIM] = acc_ref[h] / l_ref[h]


def dsa_sample_attention(page_table, q8, sel, cache_k, cache_v, layer, k_new_pad, v_new_pad):
    nb, n_pages = page_table.shape
    assert n_pages % ATTN_PAGES == 0
    width = DSA_KV_HEADS * HEAD_DIM
    per_b = lambda shape: pl.BlockSpec((None,) + shape, lambda b, j, pt: (b,) + (0,) * len(shape))
    order = lambda j, r: j * ATTN_PAGES + r
    pages = lambda: _page_specs(ATTN_PAGES, PAGE * DSA_KV_HEADS, HEAD_DIM, layer, order)
    hshape = (DSA_HEADS, QPAD, LANES)
    grid_spec = pltpu.PrefetchScalarGridSpec(
        num_scalar_prefetch=1,
        grid=(nb, n_pages // ATTN_PAGES),
        in_specs=[per_b((QPAD, DSA_HEADS * HEAD_DIM)), per_b((n_pages + 1, QPAD, LANES))]
        + pages() + pages() + [per_b((PAGE, width)), per_b((PAGE, width))],
        out_specs=per_b((QPAD, DSA_HEADS * HEAD_DIM)),
        scratch_shapes=[pltpu.VMEM((DSA_KV_HEADS, DSA_GROUP * QPAD, HEAD_DIM), BF16),
                        pltpu.VMEM(hshape, F32), pltpu.VMEM(hshape, F32), pltpu.VMEM(hshape, F32)],
    )
    return pl.pallas_call(
        functools.partial(_dsa_sample_attn_kernel, n_pages=n_pages),
        grid_spec=grid_spec,
        out_shape=jax.ShapeDtypeStruct((nb, QPAD, DSA_HEADS * HEAD_DIM), F32),
        compiler_params=_cparams(("parallel", "arbitrary"), VMEM_LIMIT),
        name="dsa_sample_attention",
    )(page_table, q8, sel, *([cache_k] * ATTN_PAGES), *([cache_v] * ATTN_PAGES), k_new_pad, v_new_pad)


def _sb_sample_kernel(pt_ref, q_ref, *refs, n_new):
    k_refs = refs[:ATTN_PAGES]
    v_refs = refs[ATTN_PAGES:2 * ATTN_PAGES]
    kn_ref, vn_ref, o_ref, qh_ref, carry_ref, acc_ref = refs[2 * ATTN_PAGES:]
    j = pl.program_id(1)
    suffix_mat = _suffix_matrix()
    rows = SB_HEADS * QPAD

    def attend(k_of, v_of, strict):
        qh = qh_ref[...]
        z = jnp.concatenate([_dot_nt(qh, k_of(h).astype(BF16))[h * QPAD:(h + 1) * QPAD]
                             for h in range(SB_HEADS)], axis=0)
        carry, w = _sb_weights(z, carry_ref[...], suffix_mat, strict)
        wb = w.astype(BF16)
        pv = jnp.concatenate([_dot(wb, v_of(h).astype(BF16))[h * QPAD:(h + 1) * QPAD]
                              for h in range(SB_HEADS)], axis=0)
        carry_ref[...] = carry
        acc_ref[...] = acc_ref[...] + pv

    @pl.when(j == 0)
    def _():
        q = q_ref[...].astype(F32)
        qh_ref[...] = jnp.concatenate(
            [q[:, h * SB_HEAD_DIM:(h + 1) * SB_HEAD_DIM] for h in range(SB_HEADS)], axis=0).astype(BF16)
        carry_ref[...] = jnp.zeros(carry_ref.shape, F32)
        acc_ref[...] = jnp.zeros(acc_ref.shape, F32)
        row = lax.broadcasted_iota(I32, (rows, LANES), 0) % QPAD
        col = lax.broadcasted_iota(I32, (rows, LANES), 1)
        strict = (col < row) & (row < n_new)
        head = lambda ref: (lambda h: ref[:, h * SB_HEAD_DIM:(h + 1) * SB_HEAD_DIM])
        attend(head(kn_ref), head(vn_ref), strict)

    gather = lambda page_refs: (lambda h: jnp.concatenate(
        [_head_rows(r, h, SB_HEADS) for r in reversed(page_refs)], axis=0))
    attend(gather(k_refs), gather(v_refs), None)

    @pl.when(j == pl.num_programs(1) - 1)
    def _():
        for h in range(SB_HEADS):
            o_ref[:, h * SB_HEAD_DIM:(h + 1) * SB_HEAD_DIM] = acc_ref[h * QPAD:(h + 1) * QPAD, :]


def sb_sample_attention(page_table, q8, cache_k, cache_v, layer, k_new_pad, v_new_pad, n_new):
    nb, n_pages = page_table.shape
    assert n_pages % ATTN_PAGES == 0
    width = SB_HEADS * SB_HEAD_DIM
    per_b = lambda shape: pl.BlockSpec((None,) + shape, lambda b, j, pt: (b,) + (0,) * len(shape))
    order = lambda j, r: n_pages - 1 - (j * ATTN_PAGES + r)
    pages = lambda: _page_specs(ATTN_PAGES, PAGE * SB_HEADS, SB_HEAD_DIM, layer, order)
    grid_spec = pltpu.PrefetchScalarGridSpec(
        num_scalar_prefetch=1,
        grid=(nb, n_pages // ATTN_PAGES),
        in_specs=[per_b((QPAD, width))] + pages() + pages() + [per_b((PAGE, width)), per_b((PAGE, width))],
        out_specs=per_b((QPAD, width)),
        scratch_shapes=[pltpu.VMEM((SB_HEADS * QPAD, SB_HEAD_DIM), BF16),
                        pltpu.VMEM((SB_HEADS * QPAD, LANES), F32),
                        pltpu.VMEM((SB_HEADS * QPAD, SB_HEAD_DIM), F32)],
    )
    return pl.pallas_call(
        functools.partial(_sb_sample_kernel, n_new=n_new),
        grid_spec=grid_spec,
        out_shape=jax.ShapeDtypeStruct((nb, QPAD, width), F32),
        compiler_params=_cparams(("parallel", "arbitrary"), VMEM_LIMIT),
        name="sb_sample_attention",
    )(page_table, q8, *([cache_k] * ATTN_PAGES), *([cache_v] * ATTN_PAGES), k_new_pad, v_new_pad)


def _pad_rows(a, nb, n_new, rows):
    a = a.reshape(nb, n_new, a.shape[-1])
    return jnp.pad(a, ((0, 0), (0, rows - n_new), (0, 0)))


def _token_head_rows(cache):
    return cache.reshape(cache.shape[:2] + (cache.shape[2] * cache.shape[3], cache.shape[4]))


def dsa_sample(page_table, q, qi, wi, k_new, v_new, ki_new, cache_k, cache_v, cache_idx_k, layer, topk):
    nb = page_table.shape[0]
    n_new = q.shape[0] // nb
    sel = dsa_sample_select(page_table, _pad_rows(qi, nb, n_new, QPAD), _pad_rows(wi, nb, n_new, QPAD),
                            cache_idx_k, layer, _pad_rows(ki_new, nb, n_new, PAGE), n_new, topk)
    o = dsa_sample_attention(page_table, _pad_rows(q, nb, n_new, QPAD), sel,
                             _token_head_rows(cache_k), _token_head_rows(cache_v), layer,
                             _pad_rows(k_new, nb, n_new, PAGE), _pad_rows(v_new, nb, n_new, PAGE))
    return o[:, :n_new].reshape(nb * n_new, -1)


def sb_sample(page_table, q, k_new, v_new, cache_k, cache_v, layer):
    nb = page_table.shape[0]
    n_new = q.shape[0] // nb
    o = sb_sample_attention(page_table, _pad_rows(q, nb, n_new, QPAD),
                            _token_head_rows(cache_k), _token_head_rows(cache_v), layer,
                            _pad_rows(k_new, nb, n_new, PAGE), _pad_rows(v_new, nb, n_new, PAGE), n_new)
    return o[:, :n_new].reshape(nb * n_new, -1)


S5_TILE_U = 128
S5_TILE_S = S5_TILE_U // S5_GROUP * S5_STATE
S5_NTILE = D_MODEL // S5_TILE_U
S5_TILES_PER_HALF = S5_NTILE // 2
S5_LANE_TILES = S5_HALF // LANES


def _s5_kernel(x_ref, g_ref, d_ref, bx_ref, ct_ref, are_ref, aim_ref, h0re_ref, h0im_ref,
               y_ref, hre_out, him_out, xre, xim, hre, him, *, nb, tc, t_last, split_input):
    c = pl.program_id(0)
    rows_half = (nb // 2) * tc

    @pl.when(c == 0)
    def _():
        hre[...] = h0re_ref[...]
        him[...] = h0im_ref[...]

    u = _rms(x_ref[...].reshape(rows_half, D_MODEL), g_ref[...])
    u_hi, u_lo = _split_bf16(u)
    lanes_per_tile = S5_TILE_S // LANES
    for tile in range(S5_NTILE):
        half, ctile = divmod(tile, S5_TILES_PER_HALF)
        rows = slice(half * rows_half, (half + 1) * rows_half)
        ucol = slice(tile * S5_TILE_U, (tile + 1) * S5_TILE_U)
        a_hi, a_lo = u_hi[:, ucol], u_lo[:, ucol]
        re = _dot(a_hi, bx_ref[0, tile])
        im = _dot(a_hi, bx_ref[2, tile])
        if split_input:
            re = re + _dot(a_lo, bx_ref[0, tile]) + _dot(a_hi, bx_ref[1, tile])
            im = im + _dot(a_lo, bx_ref[2, tile]) + _dot(a_hi, bx_ref[3, tile])
        for k in range(lanes_per_tile):
            lt = ctile * lanes_per_tile + k
            xre[lt, rows, :] = re[:, k * LANES:(k + 1) * LANES]
            xim[lt, rows, :] = im[:, k * LANES:(k + 1) * LANES]

    def step(t, carry):
        idx = pl.ds(t, nb, stride=tc)
        new = []
        for lt in range(S5_LANE_TILES):
            hr, hi = carry[lt]
            ar = are_ref[lt]
            ai = aim_ref[lt]
            nr = ar * hr - ai * hi + xre[lt, idx, :]
            ni = ar * hi + ai * hr + xim[lt, idx, :]
            xre[lt, idx, :] = nr
            xim[lt, idx, :] = ni
            new.append((nr, ni))
        return tuple(new)

    h_end = lax.fori_loop(0, tc, step, tuple((hre[lt], him[lt]) for lt in range(S5_LANE_TILES)))
    for lt in range(S5_LANE_TILES):
        hre[lt] = h_end[lt][0]
        him[lt] = h_end[lt][1]

    ys = []
    for tile in range(S5_NTILE):
        half, ctile = divmod(tile, S5_TILES_PER_HALF)
        rows = slice(half * rows_half, (half + 1) * rows_half)
        lts = range(ctile * lanes_per_tile, (ctile + 1) * lanes_per_tile)
        h_re = jnp.concatenate([xre[lt, rows, :] for lt in lts], axis=1).astype(BF16)
        h_im = jnp.concatenate([xim[lt, rows, :] for lt in lts], axis=1).astype(BF16)
        ys.append(_dot(h_re, ct_ref[0, tile]) + _dot(h_im, ct_ref[1, tile]))
    y = jnp.concatenate(ys, axis=1) + d_ref[...] * u
    y_ref[...] = y.reshape(y_ref.shape)

    @pl.when(c == t_last // tc)
    def _():
        idx = pl.ds(t_last % tc, nb, stride=tc)
        for lt in range(S5_LANE_TILES):
            hre_out[lt] = xre[lt, idx, :]
            him_out[lt] = xim[lt, idx, :]


def _s5_discretize(log_dt, a_re, a_im, b_re, b_im, c_re, c_im):
    dt = jnp.exp(log_dt)[:, None]
    decay = jnp.exp(dt * a_re)
    ang = dt * a_im
    ab_re = decay * jnp.cos(ang)
    ab_im = decay * jnp.sin(ang)
    den = a_re * a_re + a_im * a_im
    nr = ab_re - 1.0
    cf_re = (nr * a_re + ab_im * a_im) / den
    cf_im = (ab_im * a_re - nr * a_im) / den
    bx_re = cf_re[:, :, None] * b_re - cf_im[:, :, None] * b_im
    bx_im = cf_re[:, :, None] * b_im + cf_im[:, :, None] * b_re
    gpt = S5_TILE_U // S5_GROUP
    eye = jnp.eye(gpt, dtype=F32)

    def b_tiles(b):
        bt = b.reshape(S5_NTILE, gpt, S5_STATE, S5_GROUP)
        return jnp.einsum('tgpc,gh->tgchp', bt, eye).reshape(S5_NTILE, S5_TILE_U, S5_TILE_S)

    def c_tiles(cm):
        ctl = cm.reshape(S5_NTILE, gpt, S5_GROUP, S5_STATE)
        return jnp.einsum('tgcp,gh->tgphc', ctl, eye).reshape(S5_NTILE, S5_TILE_S, S5_TILE_U)

    bre_hi, bre_lo = _split_bf16(b_tiles(bx_re))
    bim_hi, bim_lo = _split_bf16(b_tiles(bx_im))
    bx = jnp.stack([bre_hi, bre_lo, bim_hi, bim_lo])
    ct = jnp.stack([c_tiles(c_re), -c_tiles(c_im)]).astype(BF16)
    return ab_re.reshape(-1), ab_im.reshape(-1), bx, ct


def _s5_rows(vec, n_seq):
    v = vec.reshape(2, 1, S5_LANE_TILES, LANES)
    return jnp.broadcast_to(v, (2, n_seq, S5_LANE_TILES, LANES)).transpose(2, 0, 1, 3).reshape(
        S5_LANE_TILES, 2 * n_seq, LANES)


def _s5_state_to_rows(h):
    n = h.shape[0]
    return h.reshape(n, 2, S5_LANE_TILES, LANES).transpose(2, 1, 0, 3).reshape(S5_LANE_TILES, 2 * n, LANES)


def _s5_rows_to_state(r):
    n = r.shape[1] // 2
    return r.reshape(S5_LANE_TILES, 2, n, LANES).transpose(2, 1, 0, 3).reshape(n, S5_GROUPS, S5_STATE)


def _s5_chunk_len(t_total):
    best = None
    for c in range(SUBLANES, QBLK + 1, 2 * SUBLANES):
        if t_total % c == 0:
            best = c
    assert best is not None
    return best


def s5_scan(x3, g, d_skip, disc, h0_re, h0_im, n_seq, tc, t_last, split_input):
    ab_re, ab_im, bx, ct = disc
    gdim, t_total, _ = x3.shape
    rchunk = n_seq * tc // gdim
    assert t_total % rchunk == 0
    nb = 2 * n_seq
    nchunks = t_total // rchunk
    st = (S5_LANE_TILES, nb, LANES)
    buf = (S5_LANE_TILES, nb * tc, LANES)
    y, hre, him = pl.pallas_call(
        functools.partial(_s5_kernel, nb=nb, tc=tc, t_last=t_last, split_input=split_input),
        grid=(nchunks,),
        in_specs=[pl.BlockSpec((gdim, rchunk, D_MODEL), lambda c: (0, c, 0)),
                  _const_spec((1, D_MODEL)), _const_spec((1, D_MODEL)),
                  _const_spec(bx.shape), _const_spec(ct.shape),
                  _const_spec(st), _const_spec(st), _const_spec(st), _const_spec(st)],
        out_specs=[pl.BlockSpec((gdim, rchunk, D_MODEL), lambda c: (0, c, 0)),
                   pl.BlockSpec(st, lambda c: (0, 0, 0)), pl.BlockSpec(st, lambda c: (0, 0, 0))],
        out_shape=[jax.ShapeDtypeStruct(x3.shape, F32),
                   jax.ShapeDtypeStruct(st, F32), jax.ShapeDtypeStruct(st, F32)],
        scratch_shapes=[pltpu.VMEM(buf, F32), pltpu.VMEM(buf, F32),
                        pltpu.VMEM(st, F32), pltpu.VMEM(st, F32)],
        compiler_params=_cparams(("arbitrary",), VMEM_LIMIT),
        name="s5_scan",
    )(x3, g.reshape(1, D_MODEL), d_skip.reshape(1, D_MODEL), bx, ct,
      _s5_rows(ab_re, n_seq), _s5_rows(ab_im, n_seq), _s5_state_to_rows(h0_re), _s5_state_to_rows(h0_im))
    return y, _s5_rows_to_state(hre), _s5_rows_to_state(him)


def kernel(x_prompt, x_sample, cache_dsa_k, cache_dsa_v, cache_dsa_idx_k, state_ssm_re, state_ssm_im,
           cache_sb_k, cache_sb_v, page_table, meta_tokens, g_mix, g_ffn,
           dsa_w_in, dsa_q_norm, dsa_k_norm, dsa_w_out,
           s5_log_dt, s5_a_re, s5_a_im, s5_b_re, s5_b_im, s5_c_re, s5_c_im, s5_d, s5_w_out,
           sb_w_in, sb_w_out, ffn_w_gu, ffn_w_down, moe_w_router, moe_b_router, moe_w_gu, moe_w_down):
    bsz, seq, _ = x_prompt.shape
    t_p = seq + N_META
    t_pad = -(-t_p // KEY_CHUNK) * KEY_CHUNK
    meta = jnp.broadcast_to(meta_tokens[None], (bsz, N_META, D_MODEL))
    xp = jnp.concatenate([meta, x_prompt, jnp.zeros((bsz, t_pad - t_p, D_MODEL), F32)], axis=1)
    xp = xp.reshape(bsz * t_pad, D_MODEL)
    nb, n_new, _ = x_sample.shape
    xs = x_sample.reshape(nb * n_new, D_MODEL)
    past_len = page_table.shape[1] * PAGE
    topk_p = min(TOPK_MAX, seq // 4)
    topk_s = min(TOPK_MAX, (past_len + n_new) // 4)
    tm_p = PROMPT_TILE
    tm_s = nb * n_new

    p3 = lambda a: a.reshape(bsz, t_pad, a.shape[-1])
    out_p = lambda a, *hd: a.reshape((bsz, t_pad) + hd)[:, :t_p]
    out_s = lambda a, *hd: a.reshape((nb, n_new) + hd)

    outs = {name: [] for name in ("dsa_k_p", "dsa_k_s", "dsa_v_p", "dsa_v_s", "dsa_i_p", "dsa_i_s",
                                  "ssm_re_p", "ssm_re_s", "ssm_im_p", "ssm_im_s",
                                  "sb_k_p", "sb_k_s", "sb_v_p", "sb_v_s")}
    ia = ib = ic = i_dense = i_moe = 0
    depth = g_mix.shape[0]
    for layer in range(depth):
        kind = layer % 3
        if kind == 0:
            w_in = jnp.concatenate(
                [dsa_w_in[ia], jnp.zeros((D_MODEL, DSA_IN_PAD - dsa_w_in.shape[2]), F32)], axis=1).astype(BF16)
            w_out = dsa_w_out[ia].astype(BF16)
            q, k, v, qi, ki, wi = dsa_project(xp, g_mix[layer], w_in, dsa_q_norm[ia], dsa_k_norm[ia], tm_p)
            o = dsa_attention_prompt(p3(q), p3(qi), p3(wi), p3(k), p3(v), p3(ki), topk_p)
            xp = out_project(xp, o.reshape(bsz * t_pad, D_MODEL), w_out, tm_p)
            qs, ks, vs, qis, kis, wis = dsa_project(xs, g_mix[layer], w_in, dsa_q_norm[ia], dsa_k_norm[ia], tm_s)
            o_s = dsa_sample(page_table, qs, qis, wis, ks, vs, kis,
                             cache_dsa_k, cache_dsa_v, cache_dsa_idx_k, ia, topk_s)
            xs = out_project(xs, o_s.astype(BF16), w_out, tm_s)
            outs["dsa_k_p"].append(out_p(k, DSA_KV_HEADS, HEAD_DIM))
            outs["dsa_v_p"].append(out_p(v, DSA_KV_HEADS, HEAD_DIM))
            outs["dsa_i_p"].append(out_p(ki, IDX_DIM))
            outs["dsa_k_s"].append(out_s(ks, DSA_KV_HEADS, HEAD_DIM))
            outs["dsa_v_s"].append(out_s(vs, DSA_KV_HEADS, HEAD_DIM))
            outs["dsa_i_s"].append(out_s(kis, IDX_DIM))
            ia += 1
        elif kind == 1:
            disc = _s5_discretize(s5_log_dt[ib], s5_a_re[ib], s5_a_im[ib], s5_b_re[ib], s5_b_im[ib],
                                  s5_c_re[ib], s5_c_im[ib])
            w_out = s5_w_out[ib].astype(BF16)
            zeros = jnp.zeros((bsz, S5_GROUPS, S5_STATE), F32)
            yp, hre_p, him_p = s5_scan(p3(xp), g_mix[layer], s5_d[ib], disc, zeros, zeros, bsz,
                                       _s5_chunk_len(t_pad), t_p - 1, False)
            xp = s5_glu(xp, yp.reshape(bsz * t_pad, D_MODEL), w_out, tm_p)
            ys, hre_s, him_s = s5_scan(xs.reshape(1, nb * n_new, D_MODEL), g_mix[layer], s5_d[ib], disc,
                                       state_ssm_re[ib], state_ssm_im[ib], nb, n_new, n_new - 1, True)
            xs = s5_glu(xs, ys.reshape(nb * n_new, D_MODEL), w_out, tm_s)
            outs["ssm_re_p"].append(hre_p)
            outs["ssm_im_p"].append(him_p)
            outs["ssm_re_s"].append(hre_s)
            outs["ssm_im_s"].append(him_s)
            ib += 1
        else:
            w_in = sb_w_in[ic].astype(BF16)
            w_out = sb_w_out[ic].astype(BF16)
            q, k, v, kb, vb = sb_project(xp, g_mix[layer], w_in, tm_p)
            o = sb_attention_prompt(p3(q), p3(kb), p3(vb))
            xp = out_project(xp, o.reshape(bsz * t_pad, D_MODEL), w_out, tm_p)
            qs, ks, vs, _, _ = sb_project(xs, g_mix[layer], w_in, tm_s)
            o_s = sb_sample(page_table, qs, ks, vs, cache_sb_k, cache_sb_v, ic)
            xs = out_project(xs, o_s.astype(BF16), w_out, tm_s)
            outs["sb_k_p"].append(out_p(k, SB_HEADS, SB_HEAD_DIM))
            outs["sb_v_p"].append(out_p(v, SB_HEADS, SB_HEAD_DIM))
            outs["sb_k_s"].append(out_s(ks, SB_HEADS, SB_HEAD_DIM))
            outs["sb_v_s"].append(out_s(vs, SB_HEADS, SB_HEAD_DIM))
            ic += 1
        if layer % 2 == 0:
            wgu = ffn_w_gu[i_dense].astype(BF16)
            wd = ffn_w_down[i_dense].astype(BF16)
            xp = ffn(xp, g_ffn[layer], wgu, wd, tm_p)
            xs = ffn(xs, g_ffn[layer], wgu, wd, tm_s)
            i_dense += 1
        else:
            wr = jnp.concatenate([moe_w_router[i_moe], jnp.zeros((D_MODEL, LANES - N_EXPERTS), F32)], axis=1)
            br = jnp.concatenate([moe_b_router[i_moe], jnp.full((LANES - N_EXPERTS,), NEG_BIG, F32)])
            br = br.reshape(1, LANES)
            wgu = moe_w_gu[i_moe].astype(BF16)
            wd = moe_w_down[i_moe].astype(BF16)
            xp = moe(xp, g_ffn[layer], wr, br, wgu, wd, tm_p)
            xs = moe(xs, g_ffn[layer], wr, br, wgu, wd, tm_s)
            i_moe += 1

    y_prompt = xp.reshape(bsz, t_pad, D_MODEL)[:, N_META:t_p]
    y_sample = xs.reshape(nb, n_new, D_MODEL)
    stack = lambda name: jnp.stack(outs[name])
    return (y_prompt, y_sample,
            stack("dsa_k_p"), stack("dsa_k_s"), stack("dsa_v_p"), stack("dsa_v_s"),
            stack("dsa_i_p"), stack("dsa_i_s"),
            stack("ssm_re_p"), stack("ssm_re_s"), stack("ssm_im_p"), stack("ssm_im_s"),
            stack("sb_k_p"), stack("sb_k_s"), stack("sb_v_p"), stack("sb_v_s"))
```

```python
import functools
import math

import jax
import jax.numpy as jnp
from jax import lax
from jax.experimental import pallas as pl
from jax.experimental.pallas import tpu as pltpu

F32 = jnp.float32
BF16 = jnp.bfloat16
I32 = jnp.int32

D_MODEL = 1024
N_META = 16
EPS = 1e-6
DSA_HEADS = 8
DSA_KV_HEADS = 2
DSA_GROUP = DSA_HEADS // DSA_KV_HEADS
HEAD_DIM = 128
IDX_HEADS = 8
IDX_DIM = 64
TOPK_MAX = 256
S5_GROUP = 16
S5_GROUPS = 64
S5_STATE = 64
S5_NSTATE = S5_GROUPS * S5_STATE
S5_HALF = S5_NSTATE // 2
SB_HEADS = 8
SB_HEAD_DIM = 128
FF_DIM = 11 * D_MODEL // 4
N_EXPERTS = 8
PAGE = 128

LANES = 128
SUBLANES = 8
VMEM_LIMIT = 56 * 1024 * 1024

QBLK = 128
KEY_SUB = 3
KEY_CHUNK = KEY_SUB * QBLK
PROMPT_TILE = 512
DSA_IN_DIM = (DSA_HEADS + 2 * DSA_KV_HEADS) * HEAD_DIM + IDX_HEADS * IDX_DIM + IDX_DIM + IDX_HEADS
DSA_IN_PAD = -(-DSA_IN_DIM // LANES) * LANES
INT_MIN = -(2 ** 31)
NEG_BIG = -1e30


def _cparams(sem, vmem=None):
    return pltpu.CompilerParams(dimension_semantics=sem, vmem_limit_bytes=vmem)


def _const_spec(shape):
    nd = len(shape)
    return pl.BlockSpec(shape, lambda *_: (0,) * nd, pipeline_mode=pl.Buffered(1))


def _rms(x, g):
    return x * lax.rsqrt(jnp.mean(x * x, axis=-1, keepdims=True) + EPS) * g


def _dot(a, b):
    return jnp.dot(a, b, preferred_element_type=F32)


def _dot_nt(a, b):
    return lax.dot_general(a, b, (((1,), (1,)), ((), ())), preferred_element_type=F32)


def _split_bf16(x):
    hi = x.astype(BF16)
    lo = (x - hi.astype(F32)).astype(BF16)
    return hi, lo


def _lane_tiles(x):
    return [x[:, u * LANES:(u + 1) * LANES] for u in range(x.shape[-1] // LANES)]


def _dsa_proj_kernel(x_ref, g_ref, w_ref, qg_ref, kg_ref,
                     q_ref, k_ref, v_ref, qi_ref, ki_ref, wi_ref):
    h = _rms(x_ref[...], g_ref[...]).astype(BF16)
    nq = DSA_HEADS * HEAD_DIM
    nkv = DSA_KV_HEADS * HEAD_DIM
    q = _dot(h, w_ref[:, 0:nq])
    scale = 1.0 / math.sqrt(HEAD_DIM)
    for hd in range(DSA_HEADS):
        sl = slice(hd * HEAD_DIM, (hd + 1) * HEAD_DIM)
        q_ref[:, sl] = (_rms(q[:, sl], qg_ref[...]) * scale).astype(BF16)
    k = _dot(h, w_ref[:, nq:nq + nkv])
    for hd in range(DSA_KV_HEADS):
        sl = slice(hd * HEAD_DIM, (hd + 1) * HEAD_DIM)
        k_ref[:, sl] = _rms(k[:, sl], kg_ref[...])
    v_ref[...] = _dot(h, w_ref[:, nq + nkv:nq + 2 * nkv])
    o = nq + 2 * nkv
    qi_ref[...] = _dot(h, w_ref[:, o:o + IDX_HEADS * IDX_DIM]).astype(BF16)
    o += IDX_HEADS * IDX_DIM
    tail = _dot(h, w_ref[:, o:o + LANES])
    ki_ref[...] = tail[:, 0:IDX_DIM]
    wi_ref[...] = tail[:, IDX_DIM:IDX_DIM + IDX_HEADS]


def dsa_project(x, g, w_pad, qg, kg, tm):
    m = x.shape[0]
    assert m % tm == 0
    nq = DSA_HEADS * HEAD_DIM
    nkv = DSA_KV_HEADS * HEAD_DIM
    row = lambda n: pl.BlockSpec((tm, n), lambda i: (i, 0))
    return pl.pallas_call(
        _dsa_proj_kernel,
        grid=(m // tm,),
        in_specs=[row(D_MODEL), _const_spec((1, D_MODEL)), _const_spec(w_pad.shape),
                  _const_spec((1, HEAD_DIM)), _const_spec((1, HEAD_DIM))],
        out_specs=[row(nq), row(nkv), row(nkv), row(IDX_HEADS * IDX_DIM), row(IDX_DIM), row(IDX_HEADS)],
        out_shape=[jax.ShapeDtypeStruct((m, nq), BF16),
                   jax.ShapeDtypeStruct((m, nkv), F32),
                   jax.ShapeDtypeStruct((m, nkv), F32),
                   jax.ShapeDtypeStruct((m, IDX_HEADS * IDX_DIM), BF16),
                   jax.ShapeDtypeStruct((m, IDX_DIM), F32),
                   jax.ShapeDtypeStruct((m, IDX_HEADS), F32)],
        compiler_params=_cparams(("parallel",), VMEM_LIMIT),
        name="dsa_project",
    )(x, g.reshape(1, D_MODEL), w_pad, qg.reshape(1, HEAD_DIM), kg.reshape(1, HEAD_DIM))


def _sb_proj_kernel(x_ref, g_ref, w_ref, q_ref, k_ref, v_ref, kb_ref, vb_ref):
    h = _rms(x_ref[...], g_ref[...]).astype(BF16)
    scale = 1.0 / math.sqrt(SB_HEAD_DIM)
    q_ref[...] = (_dot(h, w_ref[:, 0:D_MODEL]) * scale).astype(BF16)
    k = _dot(h, w_ref[:, D_MODEL:2 * D_MODEL])
    k_ref[...] = k
    kb_ref[...] = k.astype(BF16)
    v = _dot(h, w_ref[:, 2 * D_MODEL:3 * D_MODEL])
    v_ref[...] = v
    vb_ref[...] = v.astype(BF16)


def sb_project(x, g, w, tm):
    m = x.shape[0]
    assert m % tm == 0
    row = lambda n: pl.BlockSpec((tm, n), lambda i: (i, 0))
    return pl.pallas_call(
        _sb_proj_kernel,
        grid=(m // tm,),
        in_specs=[row(D_MODEL), _const_spec((1, D_MODEL)), _const_spec(w.shape)],
        out_specs=[row(D_MODEL)] * 5,
        out_shape=[jax.ShapeDtypeStruct((m, D_MODEL), BF16),
                   jax.ShapeDtypeStruct((m, D_MODEL), F32),
                   jax.ShapeDtypeStruct((m, D_MODEL), F32),
                   jax.ShapeDtypeStruct((m, D_MODEL), BF16),
                   jax.ShapeDtypeStruct((m, D_MODEL), BF16)],
        compiler_params=_cparams(("parallel",), VMEM_LIMIT),
        name="sb_project",
    )(x, g.reshape(1, D_MODEL), w)


def _out_proj_kernel(x_ref, o_ref, w_ref, y_ref):
    y_ref[...] = x_ref[...] + _dot(o_ref[...], w_ref[...])


def out_project(x, o, w, tm):
    m = x.shape[0]
    assert m % tm == 0
    row = pl.BlockSpec((tm, D_MODEL), lambda i: (i, 0))
    return pl.pallas_call(
        _out_proj_kernel,
        grid=(m // tm,),
        in_specs=[row, row, _const_spec(w.shape)],
        out_specs=row,
        out_shape=jax.ShapeDtypeStruct((m, D_MODEL), F32),
        compiler_params=_cparams(("parallel",), VMEM_LIMIT),
        name="out_project",
    )(x, o, w)


def _gelu_tanh(y):
    c = math.sqrt(2.0 / math.pi)
    return 0.5 * y * (1.0 + jnp.tanh(c * (y + 0.044715 * (y * y * y))))


def _s5_glu_kernel(x_ref, y_ref, w_ref, o_ref):
    z = _gelu_tanh(y_ref[...]).astype(BF16)
    a = _dot(z, w_ref[:, 0:D_MODEL])
    g = _dot(z, w_ref[:, D_MODEL:2 * D_MODEL])
    o_ref[...] = x_ref[...] + a * jax.nn.sigmoid(g)


def s5_glu(x, y, w, tm):
    m = x.shape[0]
    assert m % tm == 0
    row = pl.BlockSpec((tm, D_MODEL), lambda i: (i, 0))
    return pl.pallas_call(
        _s5_glu_kernel,
        grid=(m // tm,),
        in_specs=[row, row, _const_spec(w.shape)],
        out_specs=row,
        out_shape=jax.ShapeDtypeStruct((m, D_MODEL), F32),
        compiler_params=_cparams(("parallel",), VMEM_LIMIT),
        name="s5_glu",
    )(x, y, w)


FF_CHUNK = 256


def _swiglu_acc(h, wgu_ref, wd_ref, acc):
    for c in range(FF_DIM // FF_CHUNK):
        lo = c * FF_CHUNK
        g = _dot(h, wgu_ref[:, lo:lo + FF_CHUNK])
        u = _dot(h, wgu_ref[:, FF_DIM + lo:FF_DIM + lo + FF_CHUNK])
        a = (g * jax.nn.sigmoid(g) * u).astype(BF16)
        acc = acc + _dot(a, wd_ref[lo:lo + FF_CHUNK, :])
    return acc


def _ffn_kernel(x_ref, g_ref, wgu_ref, wd_ref, o_ref):
    x = x_ref[...]
    h = _rms(x, g_ref[...]).astype(BF16)
    o_ref[...] = _swiglu_acc(h, wgu_ref, wd_ref, x)


def ffn(x, g, wgu, wd, tm):
    m = x.shape[0]
    assert m % tm == 0
    row = pl.BlockSpec((tm, D_MODEL), lambda i: (i, 0))
    return pl.pallas_call(
        _ffn_kernel,
        grid=(m // tm,),
        in_specs=[row, _const_spec((1, D_MODEL)), _const_spec(wgu.shape), _const_spec(wd.shape)],
        out_specs=row,
        out_shape=jax.ShapeDtypeStruct((m, D_MODEL), F32),
        compiler_params=_cparams(("parallel",), VMEM_LIMIT),
        name="ffn",
    )(x, g.reshape(1, D_MODEL), wgu, wd)


def _moe_kernel(x_ref, g_ref, wr_ref, br_ref, wgu_ref, wd_ref, o_ref, h_scr, comb_scr):
    e = pl.program_id(1)

    @pl.when(e == 0)
    def _():
        x = x_ref[...]
        hf = _rms(x, g_ref[...])
        h_scr[...] = hf.astype(BF16)
        h_hi, h_lo = _split_bf16(hf)
        w_hi, w_lo = _split_bf16(wr_ref[...])
        logits = _dot(h_hi, w_hi) + _dot(h_lo, w_hi) + _dot(h_hi, w_lo) + br_ref[...]
        lane = lax.broadcasted_iota(I32, logits.shape, 1)
        m1 = jnp.max(logits, axis=-1, keepdims=True)
        i1 = jnp.min(jnp.where(logits == m1, lane, LANES), axis=-1, keepdims=True)
        rest = jnp.where(lane == i1, NEG_BIG, logits)
        m2 = jnp.max(rest, axis=-1, keepdims=True)
        i2 = jnp.min(jnp.where(rest == m2, lane, LANES), axis=-1, keepdims=True)
        e2 = jnp.exp(m2 - m1)
        g1 = 1.0 / (1.0 + e2)
        g2 = e2 / (1.0 + e2)
        comb_scr[...] = jnp.where(lane == i1, g1, 0.0) + jnp.where(lane == i2, g2, 0.0)
        o_ref[...] = x

    lane = lax.broadcasted_iota(I32, comb_scr.shape, 1)
    gate = jnp.sum(jnp.where(lane == e, comb_scr[...], 0.0), axis=-1, keepdims=True)
    y = _swiglu_acc(h_scr[...], wgu_ref.at[0], wd_ref.at[0], jnp.zeros(o_ref.shape, F32))
    o_ref[...] = o_ref[...] + gate * y


def moe(x, g, w_router_pad, b_router_pad, wgu, wd, tm):
    m = x.shape[0]
    assert m % tm == 0
    row = pl.BlockSpec((tm, D_MODEL), lambda i, e: (i, 0))
    cst = lambda shape: pl.BlockSpec(shape, lambda i, e: (0,) * len(shape), pipeline_mode=pl.Buffered(1))
    return pl.pallas_call(
        _moe_kernel,
        grid=(m // tm, N_EXPERTS),
        in_specs=[row, cst((1, D_MODEL)), cst((D_MODEL, LANES)), cst((1, LANES)),
                  pl.BlockSpec((1, D_MODEL, 2 * FF_DIM), lambda i, e: (e, 0, 0)),
                  pl.BlockSpec((1, FF_DIM, D_MODEL), lambda i, e: (e, 0, 0))],
        out_specs=row,
        out_shape=jax.ShapeDtypeStruct((m, D_MODEL), F32),
        scratch_shapes=[pltpu.VMEM((tm, D_MODEL), BF16), pltpu.VMEM((tm, LANES), F32)],
        compiler_params=_cparams(("parallel", "arbitrary"), VMEM_LIMIT),
        name="moe",
    )(x, g.reshape(1, D_MODEL), w_router_pad, b_router_pad, wgu, wd)


MOE_TILE = 512
ROUTE_E1, ROUTE_E2, ROUTE_G1, ROUTE_G2 = 0, 1, 2, 3


def _route_kernel(x_ref, g_ref, wr_ref, br_ref, o_ref):
    hf = _rms(x_ref[...], g_ref[...])
    h_hi, h_lo = _split_bf16(hf)
    w_hi, w_lo = _split_bf16(wr_ref[...])
    logits = _dot(h_hi, w_hi) + _dot(h_lo, w_hi) + _dot(h_hi, w_lo) + br_ref[...]
    lane = lax.broadcasted_iota(I32, logits.shape, 1)
    m1 = jnp.max(logits, axis=-1, keepdims=True)
    i1 = jnp.min(jnp.where(logits == m1, lane, LANES), axis=-1, keepdims=True)
    rest = jnp.where(lane == i1, NEG_BIG, logits)
    m2 = jnp.max(rest, axis=-1, keepdims=True)
    i2 = jnp.min(jnp.where(rest == m2, lane, LANES), axis=-1, keepdims=True)
    e2 = jnp.exp(m2 - m1)
    g1 = 1.0 / (1.0 + e2)
    g2 = e2 / (1.0 + e2)
    o_ref[...] = (jnp.where(lane == ROUTE_E1, i1.astype(F32), 0.0) + jnp.where(lane == ROUTE_E2, i2.astype(F32), 0.0)
                  + jnp.where(lane == ROUTE_G1, g1, 0.0) + jnp.where(lane == ROUTE_G2, g2, 0.0))


def moe_route(x, g, w_router_pad, b_router_pad, tm):
    m = x.shape[0]
    assert m % tm == 0
    return pl.pallas_call(
        _route_kernel,
        grid=(m // tm,),
        in_specs=[pl.BlockSpec((tm, D_MODEL), lambda i: (i, 0)), _const_spec((1, D_MODEL)),
                  _const_spec((D_MODEL, LANES)), _const_spec((1, LANES))],
        out_specs=pl.BlockSpec((tm, LANES), lambda i: (i, 0)),
        out_shape=jax.ShapeDtypeStruct((m, LANES), F32),
        compiler_params=_cparams(("parallel",), VMEM_LIMIT),
        name="moe_route",
    )(x, g.reshape(1, D_MODEL), w_router_pad, b_router_pad)


def _row_gather_start(idx_ref, n_rows, src_hbm, dst, sem):
    def body(r, _):
        pltpu.make_async_copy(src_hbm.at[pl.ds(idx_ref[0, r], 1)], dst.at[pl.ds(r, 1)], sem).start()
        return 0
    lax.fori_loop(0, n_rows, body, 0, unroll=8)


def _row_gather_wait(n_rows, src_hbm, dst, sem):
    pltpu.make_async_copy(src_hbm.at[pl.ds(0, n_rows)], dst, sem).wait()


def _moe_expert_kernel(te_ref, nu_ref, src_cur, src_next, x_hbm, g_ref, wgu_ref, wd_ref, o_ref, xbuf, sem):
    i = pl.program_id(0)
    slot = i % 2
    n_used = nu_ref[0]
    tm = o_ref.shape[0]

    @pl.when(i == 0)
    def _():
        _row_gather_start(src_cur, tm, x_hbm, xbuf.at[0], sem.at[0])

    @pl.when(i + 1 < n_used)
    def _():
        _row_gather_start(src_next, tm, x_hbm, xbuf.at[1 - slot], sem.at[1 - slot])

    @pl.when(i < n_used)
    def _():
        _row_gather_wait(tm, x_hbm, xbuf.at[slot], sem.at[slot])
        h = _rms(xbuf[slot], g_ref[...]).astype(BF16)
        o_ref[...] = _swiglu_acc(h, wgu_ref.at[0], wd_ref.at[0], jnp.zeros(o_ref.shape, F32))

    @pl.when(i >= n_used)
    def _():
        o_ref[...] = jnp.zeros(o_ref.shape, F32)


def moe_experts(x, g, src_rows, tile_expert, n_used, wgu, wd, tm):
    n_tiles = src_rows.shape[0]
    cst = lambda shape: pl.BlockSpec(shape, lambda i, te, nu: (0,) * len(shape), pipeline_mode=pl.Buffered(1))
    smem_rows = lambda off: pl.BlockSpec((None, 1, tm), lambda i, te, nu: (jnp.minimum(i + off, n_tiles - 1), 0, 0),
                                         memory_space=pltpu.SMEM)
    grid_spec = pltpu.PrefetchScalarGridSpec(
        num_scalar_prefetch=2,
        grid=(n_tiles,),
        in_specs=[smem_rows(0), smem_rows(1), pl.BlockSpec(memory_space=pl.ANY), cst((1, D_MODEL)),
                  pl.BlockSpec((1, D_MODEL, 2 * FF_DIM), lambda i, te, nu: (te[i], 0, 0)),
                  pl.BlockSpec((1, FF_DIM, D_MODEL), lambda i, te, nu: (te[i], 0, 0))],
        out_specs=pl.BlockSpec((tm, D_MODEL), lambda i, te, nu: (i, 0)),
        scratch_shapes=[pltpu.VMEM((2, tm, D_MODEL), F32), pltpu.SemaphoreType.DMA((2,))],
    )
    return pl.pallas_call(
        _moe_expert_kernel,
        grid_spec=grid_spec,
        out_shape=jax.ShapeDtypeStruct((n_tiles * tm, D_MODEL), F32),
        compiler_params=_cparams(("arbitrary",), VMEM_LIMIT),
        name="moe_experts",
    )(tile_expert, n_used, src_rows, src_rows, x, g.reshape(1, D_MODEL), wgu, wd)


def _moe_combine_kernel(dst_cur, dst_next, x_ref, route_ref, y_hbm, o_ref, ybuf, sem):
    i = pl.program_id(0)
    slot = i % 2
    tm = o_ref.shape[0]
    rows = 2 * tm

    @pl.when(i == 0)
    def _():
        _row_gather_start(dst_cur, rows, y_hbm, ybuf.at[0], sem.at[0])

    @pl.when(i + 1 < pl.num_programs(0))
    def _():
        _row_gather_start(dst_next, rows, y_hbm, ybuf.at[1 - slot], sem.at[1 - slot])

    _row_gather_wait(rows, y_hbm, ybuf.at[slot], sem.at[slot])
    route = route_ref[...]
    g1 = route[:, ROUTE_G1:ROUTE_G1 + 1]
    g2 = route[:, ROUTE_G2:ROUTE_G2 + 1]
    o_ref[...] = x_ref[...] + g1 * ybuf[slot, 0:tm, :] + g2 * ybuf[slot, tm:rows, :]


def moe_combine(x, route, y_sorted, dest_rows, tm):
    m = x.shape[0]
    assert m % tm == 0
    n = m // tm
    smem_rows = lambda off: pl.BlockSpec((None, 1, 2 * tm), lambda i: (jnp.minimum(i + off, n - 1), 0, 0),
                                         memory_space=pltpu.SMEM)
    return pl.pallas_call(
        _moe_combine_kernel,
        grid=(n,),
        in_specs=[smem_rows(0), smem_rows(1), pl.BlockSpec((tm, D_MODEL), lambda i: (i, 0)),
                  pl.BlockSpec((tm, LANES), lambda i: (i, 0)), pl.BlockSpec(memory_space=pl.ANY)],
        out_specs=pl.BlockSpec((tm, D_MODEL), lambda i: (i, 0)),
        out_shape=jax.ShapeDtypeStruct((m, D_MODEL), F32),
        scratch_shapes=[pltpu.VMEM((2, 2 * tm, D_MODEL), F32), pltpu.SemaphoreType.DMA((2,))],
        compiler_params=_cparams(("arbitrary",), VMEM_LIMIT),
        name="moe_combine",
    )(dest_rows, dest_rows, x, route, y_sorted)


def moe_routed(x, g, w_router_pad, b_router_pad, wgu, wd, tm):
    m = x.shape[0]
    route = moe_route(x, g, w_router_pad, b_router_pad, tm)
    experts = route[:, :2].astype(I32)
    flat = experts.reshape(-1)
    onehot = (flat[:, None] == jnp.arange(N_EXPERTS, dtype=I32)[None, :]).astype(I32)
    csum = jnp.cumsum(onehot, axis=0)
    rank = jnp.take_along_axis(csum, flat[:, None], axis=1)[:, 0] - 1
    tiles_per_expert = (csum[-1] + tm - 1) // tm
    tile_end = jnp.cumsum(tiles_per_expert)
    row_start = (tile_end - tiles_per_expert) * tm
    dest = row_start[flat] + rank
    n_tiles = 2 * m // tm + N_EXPERTS
    src = jnp.zeros((n_tiles * tm,), I32).at[dest].set(jnp.arange(2 * m, dtype=I32) // 2, unique_indices=True)
    tile_expert = jnp.minimum(jnp.searchsorted(tile_end, jnp.arange(n_tiles, dtype=I32), side="right"),
                              N_EXPERTS - 1).astype(I32)
    n_used = tile_end[-1:].astype(I32)
    y_sorted = moe_experts(x, g, src.reshape(n_tiles, 1, tm), tile_expert, n_used, wgu, wd, tm)
    dest_rows = dest.reshape(m // tm, tm, 2).transpose(0, 2, 1).reshape(m // tm, 1, 2 * tm)
    return moe_combine(x, route, y_sorted, dest_rows, tm)


NEG_INF_CODE = INT_MIN + 0x7FFFFF


def _code_to_float(code):
    return pltpu.bitcast(code ^ ((code >> 31) & 0x7FFFFFFF), F32)


def _kth_largest(count_ge, topk, shape):
    zero = jnp.zeros(shape, I32)
    c0 = jnp.where(count_ge(_code_to_float(zero)) >= topk, zero, jnp.full(shape, INT_MIN, I32))

    def body(k, code):
        cand = code | jnp.left_shift(jnp.int32(1), 30 - k)
        return jnp.where(count_ge(_code_to_float(cand)) >= topk, cand, code)

    code = lax.fori_loop(0, 31, body, c0)
    return _code_to_float(jnp.maximum(code, NEG_INF_CODE))


def _indexer_scores(lg, wib_ref, rows):
    tiles = []
    for u in range(lg.shape[1] // LANES):
        cols = slice(u * LANES, (u + 1) * LANES)
        su = wib_ref[0] * jnp.maximum(lg[0:rows, cols], 0.0)
        for h in range(1, IDX_HEADS):
            su = su + wib_ref[h] * jnp.maximum(lg[h * rows:(h + 1) * rows, cols], 0.0)
        tiles.append(su)
    return tiles


def _count_matrix():
    r = lax.broadcasted_iota(I32, (QBLK, 2 * QBLK), 0)
    c = lax.broadcasted_iota(I32, (QBLK, 2 * QBLK), 1)
    return jnp.where((r < c) | (c >= QBLK), 1.0, 0.0).astype(BF16)


def _select_chunk(s, valid, t, need, carry, count_mat):
    eq = s == t
    cnt = _dot(jnp.where(eq, 1.0, 0.0).astype(BF16), count_mat)
    before = cnt[:, :QBLK] + carry
    sel = ((s > t) | (eq & (before < need))) & valid
    return sel, carry + cnt[:, QBLK:]


def _softmax_update(groups, sels, m_ref, l_ref, acc_ref, rows):
    old = {hd: (m_ref[hd], l_ref[hd], acc_ref[hd]) for _, _, heads in groups for hd in heads}
    new = {}
    for s, v, heads in groups:
        ps = []
        m_news = []
        for j, hd in enumerate(heads):
            tiles = _lane_tiles(s[j * rows:(j + 1) * rows])
            mx = jnp.where(sels[0], tiles[0], NEG_BIG)
            for sel, tile in zip(sels[1:], tiles[1:]):
                mx = jnp.maximum(mx, jnp.where(sel, tile, NEG_BIG))
            m_new = jnp.maximum(old[hd][0], jnp.max(mx, axis=-1, keepdims=True))
            p = [jnp.where(sel, jnp.exp(tile - m_new), 0.0) for sel, tile in zip(sels, tiles)]
            ps.append(jnp.concatenate(p, axis=1).astype(BF16) if len(p) > 1 else p[0].astype(BF16))
            m_news.append(m_new)
        pv = _dot(jnp.concatenate(ps, axis=0), jnp.concatenate([v, jnp.ones(v.shape, BF16)], axis=1))
        for j, hd in enumerate(heads):
            m_old, l_old, acc_old = old[hd]
            blk = pv[j * rows:(j + 1) * rows]
            alpha = jnp.exp(m_old - m_news[j])
            new[hd] = (m_news[j], alpha * l_old + blk[:, LANES:], alpha * acc_old + blk[:, :LANES])
    for hd, (m_new, l_new, acc_new) in new.items():
        m_ref[hd] = m_new
        l_ref[hd] = l_new
        acc_ref[hd] = acc_new


def _dsa_prompt_kernel(q_ref, qi_ref, wi_ref, k_ref, v_ref, ki_ref, o_ref,
                       score_ref, wib_ref, t_ref, need_ref, m_ref, l_ref, acc_ref, *, topk):
    i = pl.program_id(1)
    nsup = i // KEY_SUB + 1
    shape = (QBLK, QBLK)
    row_pos = i * QBLK + lax.broadcasted_iota(I32, shape, 0)
    col_iota = lax.broadcasted_iota(I32, shape, 1)

    def causal(chunk):
        return chunk * QBLK + col_iota <= row_pos

    qi = qi_ref[...]
    qih = jnp.concatenate([qi[:, h * IDX_DIM:(h + 1) * IDX_DIM] for h in range(IDX_HEADS)], axis=0)
    wi = wi_ref[...]
    for h in range(IDX_HEADS):
        wib_ref[h] = jnp.broadcast_to(wi[:, h:h + 1], shape)

    def score_super(sc, _):
        start = pl.multiple_of(sc * KEY_CHUNK, KEY_CHUNK)
        kic = ki_ref[pl.ds(start, KEY_CHUNK), :].astype(BF16)
        for u, su in enumerate(_indexer_scores(_dot_nt(qih, kic), wib_ref, QBLK)):
            chunk = sc * KEY_SUB + u
            score_ref[chunk] = jnp.where(causal(chunk), su, -jnp.inf)
        return 0

    lax.fori_loop(0, nsup, score_super, 0)

    def count(x, strict):
        def body(sc, acc):
            for u in range(KEY_SUB):
                s = score_ref[sc * KEY_SUB + u]
                acc = acc + jnp.where((s > x) if strict else (s >= x), 1.0, 0.0)
            return acc
        acc = lax.fori_loop(0, nsup, body, jnp.zeros(shape, F32))
        return jnp.sum(acc, axis=-1, keepdims=True)

    @pl.when((i + 1) * QBLK <= topk)
    def _():
        t_ref[...] = jnp.full(shape, -jnp.inf, F32)
        need_ref[...] = jnp.full(shape, float(topk), F32)

    @pl.when((i + 1) * QBLK > topk)
    def _():
        t = _kth_largest(lambda x: count(x, False), float(topk), shape)
        t_ref[...] = t
        need_ref[...] = jnp.broadcast_to(float(topk) - count(t, True), shape)

    q = q_ref[...]
    qn = [jnp.concatenate([q[:, (n * DSA_GROUP + g) * HEAD_DIM:(n * DSA_GROUP + g + 1) * HEAD_DIM]
                           for g in range(DSA_GROUP)], axis=0) for n in range(DSA_KV_HEADS)]
    m_ref[...] = jnp.full(m_ref.shape, NEG_BIG, F32)
    l_ref[...] = jnp.zeros(l_ref.shape, F32)
    acc_ref[...] = jnp.zeros(acc_ref.shape, F32)
    count_mat = _count_matrix()
    t = t_ref[...]
    need = need_ref[...]

    def attn_super(sc, carry):
        start = pl.multiple_of(sc * KEY_CHUNK, KEY_CHUNK)
        sels = []
        for u in range(KEY_SUB):
            chunk = sc * KEY_SUB + u
            sel, carry = _select_chunk(score_ref[chunk], causal(chunk), t, need, carry, count_mat)
            sels.append(sel)
        kc = k_ref[pl.ds(start, KEY_CHUNK), :].astype(BF16)
        vc = v_ref[pl.ds(start, KEY_CHUNK), :].astype(BF16)
        groups = []
        for n in range(DSA_KV_HEADS):
            sl = slice(n * HEAD_DIM, (n + 1) * HEAD_DIM)
            heads = [n * DSA_GROUP + g for g in range(DSA_GROUP)]
            groups.append((_dot_nt(qn[n], kc[:, sl]), vc[:, sl], heads))
        _softmax_update(groups, sels, m_ref, l_ref, acc_ref, QBLK)
        return carry

    lax.fori_loop(0, nsup, attn_super, jnp.zeros(shape, F32))
    for h in range(DSA_HEADS):
        o_ref[:, h * HEAD_DIM:(h + 1) * HEAD_DIM] = (acc_ref[h] / l_ref[h]).astype(BF16)


def dsa_attention_prompt(q, qi, wi, k, v, ki, topk):
    b, t, _ = q.shape
    assert t % KEY_CHUNK == 0
    nq = t // QBLK
    blk = lambda n: pl.BlockSpec((None, QBLK, n), lambda bi, i: (bi, i, 0))
    full = lambda n: pl.BlockSpec((None, t, n), lambda bi, i: (bi, 0, 0))
    hshape = (DSA_HEADS, QBLK, QBLK)
    return pl.pallas_call(
        functools.partial(_dsa_prompt_kernel, topk=topk),
        grid=(b, nq),
        in_specs=[blk(q.shape[2]), blk(qi.shape[2]), blk(wi.shape[2]),
                  full(k.shape[2]), full(v.shape[2]), full(ki.shape[2])],
        out_specs=blk(q.shape[2]),
        out_shape=jax.ShapeDtypeStruct(q.shape, BF16),
        scratch_shapes=[pltpu.VMEM((nq, QBLK, QBLK), F32),
                        pltpu.VMEM((IDX_HEADS, QBLK, QBLK), F32),
                        pltpu.VMEM((QBLK, QBLK), F32),
                        pltpu.VMEM((QBLK, QBLK), F32),
                        pltpu.VMEM(hshape, F32), pltpu.VMEM(hshape, F32), pltpu.VMEM(hshape, F32)],
        compiler_params=_cparams(("parallel", "arbitrary"), VMEM_LIMIT),
        name="dsa_attention_prompt",
    )(q, qi, wi, k, v, ki)


def _suffix_matrix():
    r = lax.broadcasted_iota(I32, (QBLK, 2 * QBLK), 0)
    c = lax.broadcasted_iota(I32, (QBLK, 2 * QBLK), 1)
    return jnp.where((r > c) | (c >= QBLK), 1.0, 0.0).astype(BF16)


def _sb_weights(z, carry, suffix_mat, strict):
    rows = z.shape[0]
    lk = -(jnp.maximum(z, 0.0) + jnp.log(1.0 + jnp.exp(-jnp.abs(z))))
    lkm = lk if strict is None else jnp.where(strict, lk, 0.0)
    hi, lo = _split_bf16(lkm)
    his, los = _lane_tiles(hi), _lane_tiles(lo)
    nsub = len(his)
    sufs = [None] * nsub
    for u in reversed(range(nsub)):
        r2 = _dot(jnp.concatenate([his[u], los[u]], axis=0), suffix_mat)
        ru = r2[:rows] + r2[rows:]
        sufs[u] = ru[:, :QBLK] + carry
        carry = carry + ru[:, QBLK:]
    suffix = jnp.concatenate(sufs, axis=1) if nsub > 1 else sufs[0]
    w = jnp.exp(z + lk + suffix)
    if strict is not None:
        w = jnp.where(strict, w, 0.0)
    return carry, w


SB_HEADS_PER_STEP = 4


def _sb_prompt_kernel(q_ref, k_ref, v_ref, o_ref, carry_ref, acc_ref):
    i = pl.program_id(2)
    top = i // KEY_SUB
    q = q_ref[...]
    suffix_mat = _suffix_matrix()
    shape = (QBLK, KEY_CHUNK)
    row_pos = i * QBLK + lax.broadcasted_iota(I32, shape, 0)
    col_iota = lax.broadcasted_iota(I32, shape, 1)
    carry_ref[...] = jnp.zeros(carry_ref.shape, F32)
    acc_ref[...] = jnp.zeros(acc_ref.shape, F32)

    def run(sc, masked):
        start = pl.multiple_of(sc * KEY_CHUNK, KEY_CHUNK)
        kc = k_ref[pl.ds(start, KEY_CHUNK), :]
        vc = v_ref[pl.ds(start, KEY_CHUNK), :]
        strict = (start + col_iota < row_pos) if masked else None
        old = [(carry_ref[h], acc_ref[h]) for h in range(SB_HEADS_PER_STEP)]
        new = []
        for h in range(SB_HEADS_PER_STEP):
            sl = slice(h * SB_HEAD_DIM, (h + 1) * SB_HEAD_DIM)
            carry, w = _sb_weights(_dot_nt(q[:, sl], kc[:, sl]), old[h][0], suffix_mat, strict)
            new.append((carry, old[h][1] + _dot(w.astype(BF16), vc[:, sl])))
        for h in range(SB_HEADS_PER_STEP):
            carry_ref[h] = new[h][0]
            acc_ref[h] = new[h][1]

    run(top, True)

    def body(j, _):
        run(top - 1 - j, False)
        return 0

    lax.fori_loop(0, top, body, 0)
    for h in range(SB_HEADS_PER_STEP):
        o_ref[:, h * SB_HEAD_DIM:(h + 1) * SB_HEAD_DIM] = acc_ref[h].astype(BF16)


def sb_attention_prompt(q, k, v):
    b, t, _ = q.shape
    assert t % KEY_CHUNK == 0
    width = SB_HEADS_PER_STEP * SB_HEAD_DIM
    blk = pl.BlockSpec((None, QBLK, width), lambda bi, h, i: (bi, i, h))
    full = pl.BlockSpec((None, t, width), lambda bi, h, i: (bi, 0, h))
    hshape = (SB_HEADS_PER_STEP, QBLK, QBLK)
    return pl.pallas_call(
        _sb_prompt_kernel,
        grid=(b, SB_HEADS // SB_HEADS_PER_STEP, t // QBLK),
        in_specs=[blk, full, full],
        out_specs=blk,
        out_shape=jax.ShapeDtypeStruct(q.shape, BF16),
        scratch_shapes=[pltpu.VMEM(hshape, F32), pltpu.VMEM(hshape, F32)],
        compiler_params=_cparams(("parallel", "parallel", "arbitrary"), VMEM_LIMIT),
        name="sb_attention_prompt",
    )(q, k, v)


SELECT_PAGES = 16
ATTN_PAGES = 8
QPAD = SUBLANES


def _page_specs(n_slots, rows, width, layer, order):
    def spec(r):
        return pl.BlockSpec((None, None, rows, width),
                            lambda b, j, pt: (layer, pt[b, order(j, r)], 0, 0))
    return [spec(r) for r in range(n_slots)]


def _head_rows(page_ref, head, n_heads):
    return page_ref[pl.ds(head, PAGE, stride=n_heads), :]


def _dsa_sample_select_kernel(pt_ref, qi_ref, wi_ref, *refs, n_pages, n_new, topk):
    ki_refs = refs[:SELECT_PAGES]
    kin_ref, sel_ref, score_ref, wib_ref, qih_ref = refs[SELECT_PAGES:]
    j = pl.program_id(1)
    shape = (QPAD, LANES)

    @pl.when(j == 0)
    def _():
        qi = qi_ref[...].astype(F32)
        qih_ref[...] = jnp.concatenate(
            [qi[:, h * IDX_DIM:(h + 1) * IDX_DIM] for h in range(IDX_HEADS)], axis=0).astype(BF16)
        wi = wi_ref[...]
        for h in range(IDX_HEADS):
            wib_ref[h] = jnp.broadcast_to(wi[:, h:h + 1], shape)

    def scores(ki_t):
        lg = _dot(qih_ref[...], ki_t.astype(BF16))
        return _indexer_scores(lg, wib_ref, QPAD)

    for r, su in enumerate(scores(jnp.concatenate([r[...] for r in ki_refs], axis=1))):
        score_ref[j * SELECT_PAGES + r] = su

    @pl.when(j == pl.num_programs(1) - 1)
    def _():
        row = lax.broadcasted_iota(I32, shape, 0)
        col = lax.broadcasted_iota(I32, shape, 1)
        visible = (col <= row) & (row < n_new)
        score_ref[n_pages] = jnp.where(visible, scores(kin_ref[...])[0], -jnp.inf)
        nchunk = n_pages + 1

        def count(x, strict):
            accs = [jnp.zeros(shape, F32) for _ in range(4)]
            for c in range(nchunk):
                s = score_ref[c]
                accs[c % 4] = accs[c % 4] + jnp.where((s > x) if strict else (s >= x), 1.0, 0.0)
            return jnp.sum((accs[0] + accs[1]) + (accs[2] + accs[3]), axis=-1, keepdims=True)

        t = _kth_largest(lambda x: count(x, False), float(topk), shape)
        need = jnp.broadcast_to(float(topk) - count(t, True), shape)

        tiles = [score_ref[c] for c in range(nchunk)]
        eqf = [jnp.where(s == t, 1.0, 0.0) for s in tiles]
        eqf += [jnp.zeros(shape, F32)] * (-nchunk % 2)
        cnt = _dot(jnp.concatenate(eqf, axis=0).astype(BF16), _count_matrix())
        carry = jnp.zeros(shape, F32)
        for c in range(nchunk):
            cc = cnt[c * QPAD:(c + 1) * QPAD]
            sel = (tiles[c] > t) | ((tiles[c] == t) & (cc[:, :QBLK] + carry < need))
            if c == n_pages:
                sel = sel & visible
            sel_ref[c] = jnp.where(sel, 1.0, 0.0)
            carry = carry + cc[:, QBLK:]


def dsa_sample_select(page_table, qi8, wi8, cache_idx_k_t, layer, ki_new_t, n_new, topk):
    nb, n_pages = page_table.shape
    assert n_pages % SELECT_PAGES == 0
    per_b = lambda shape: pl.BlockSpec((None,) + shape, lambda b, j, pt: (b,) + (0,) * len(shape))
    grid_spec = pltpu.PrefetchScalarGridSpec(
        num_scalar_prefetch=1,
        grid=(nb, n_pages // SELECT_PAGES),
        in_specs=[per_b((QPAD, IDX_HEADS * IDX_DIM)), per_b((QPAD, IDX_HEADS))]
        + _page_specs(SELECT_PAGES, IDX_DIM, PAGE, layer, lambda j, r: j * SELECT_PAGES + r)
        + [per_b((IDX_DIM, PAGE))],
        out_specs=per_b((n_pages + 1, QPAD, LANES)),
        scratch_shapes=[pltpu.VMEM((n_pages + 1, QPAD, LANES), F32),
                        pltpu.VMEM((IDX_HEADS, QPAD, LANES), F32),
                        pltpu.VMEM((IDX_HEADS * QPAD, IDX_DIM), BF16)],
    )
    return pl.pallas_call(
        functools.partial(_dsa_sample_select_kernel, n_pages=n_pages, n_new=n_new, topk=topk),
        grid_spec=grid_spec,
        out_shape=jax.ShapeDtypeStruct((nb, n_pages + 1, QPAD, LANES), F32),
        compiler_params=_cparams(("parallel", "arbitrary"), VMEM_LIMIT),
        name="dsa_sample_select",
    )(page_table, qi8, wi8, *([cache_idx_k_t] * SELECT_PAGES), ki_new_t)


def _dsa_sample_attn_kernel(pt_ref, q_ref, sel_ref, *refs, n_pages):
    k_refs = refs[:ATTN_PAGES]
    v_refs = refs[ATTN_PAGES:2 * ATTN_PAGES]
    kn_ref, vn_ref, o_ref, qn_ref, m_ref, l_ref, acc_ref = refs[2 * ATTN_PAGES:]
    j = pl.program_id(1)

    @pl.when(j == 0)
    def _():
        q = q_ref[...].astype(F32)
        for n in range(DSA_KV_HEADS):
            qn_ref[n] = jnp.concatenate(
                [q[:, (n * DSA_GROUP + g) * HEAD_DIM:(n * DSA_GROUP + g + 1) * HEAD_DIM]
                 for g in range(DSA_GROUP)], axis=0).astype(BF16)
        m_ref[...] = jnp.full(m_ref.shape, NEG_BIG, F32)
        l_ref[...] = jnp.zeros(l_ref.shape, F32)
        acc_ref[...] = jnp.zeros(acc_ref.shape, F32)

    def attend(k_of, v_of, sels):
        groups = []
        for n in range(DSA_KV_HEADS):
            heads = [n * DSA_GROUP + g for g in range(DSA_GROUP)]
            groups.append((_dot_nt(qn_ref[n], k_of(n).astype(BF16)), v_of(n).astype(BF16), heads))
        _softmax_update(groups, sels, m_ref, l_ref, acc_ref, QPAD)

    gather = lambda page_refs: (lambda n: jnp.concatenate(
        [_head_rows(r, n, DSA_KV_HEADS) for r in page_refs], axis=0))
    attend(gather(k_refs), gather(v_refs), [sel_ref[j * ATTN_PAGES + r] > 0.5 for r in range(ATTN_PAGES)])

    @pl.when(j == pl.num_programs(1) - 1)
    def _():
        head = lambda ref: (lambda n: ref[:, n * HEAD_DIM:(n + 1) * HEAD_DIM])
        attend(head(kn_ref), head(vn_ref), [sel_ref[n_pages] > 0.5])
        for h in range(DSA_HEADS):
            o_ref[:, h * HEAD_DIM:(h + 1) * HEAD_DIM] = acc_ref[h] / l_ref[h]


def dsa_sample_attention(page_table, q8, sel, cache_k, cache_v, layer, k_new_pad, v_new_pad):
    nb, n_pages = page_table.shape
    assert n_pages % ATTN_PAGES == 0
    width = DSA_KV_HEADS * HEAD_DIM
    per_b = lambda shape: pl.BlockSpec((None,) + shape, lambda b, j, pt: (b,) + (0,) * len(shape))
    order = lambda j, r: j * ATTN_PAGES + r
    pages = lambda: _page_specs(ATTN_PAGES, PAGE * DSA_KV_HEADS, HEAD_DIM, layer, order)
    hshape = (DSA_HEADS, QPAD, LANES)
    grid_spec = pltpu.PrefetchScalarGridSpec(
        num_scalar_prefetch=1,
        grid=(nb, n_pages // ATTN_PAGES),
        in_specs=[per_b((QPAD, DSA_HEADS * HEAD_DIM)), per_b((n_pages + 1, QPAD, LANES))]
        + pages() + pages() + [per_b((PAGE, width)), per_b((PAGE, width))],
        out_specs=per_b((QPAD, DSA_HEADS * HEAD_DIM)),
        scratch_shapes=[pltpu.VMEM((DSA_KV_HEADS, DSA_GROUP * QPAD, HEAD_DIM), BF16),
                        pltpu.VMEM(hshape, F32), pltpu.VMEM(hshape, F32), pltpu.VMEM(hshape, F32)],
    )
    return pl.pallas_call(
        functools.partial(_dsa_sample_attn_kernel, n_pages=n_pages),
        grid_spec=grid_spec,
        out_shape=jax.ShapeDtypeStruct((nb, QPAD, DSA_HEADS * HEAD_DIM), F32),
        compiler_params=_cparams(("parallel", "arbitrary"), VMEM_LIMIT),
        name="dsa_sample_attention",
    )(page_table, q8, sel, *([cache_k] * ATTN_PAGES), *([cache_v] * ATTN_PAGES), k_new_pad, v_new_pad)


def _sb_sample_kernel(pt_ref, q_ref, *refs, n_new):
    k_refs = refs[:ATTN_PAGES]
    v_refs = refs[ATTN_PAGES:2 * ATTN_PAGES]
    kn_ref, vn_ref, o_ref, qh_ref, carry_ref, acc_ref = refs[2 * ATTN_PAGES:]
    j = pl.program_id(1)
    suffix_mat = _suffix_matrix()
    rows = SB_HEADS * QPAD

    def attend(k_of, v_of, strict):
        qh = qh_ref[...]
        z = jnp.concatenate([_dot_nt(qh, k_of(h).astype(BF16))[h * QPAD:(h + 1) * QPAD]
                             for h in range(SB_HEADS)], axis=0)
        carry, w = _sb_weights(z, carry_ref[...], suffix_mat, strict)
        wb = w.astype(BF16)
        pv = jnp.concatenate([_dot(wb, v_of(h).astype(BF16))[h * QPAD:(h + 1) * QPAD]
                              for h in range(SB_HEADS)], axis=0)
        carry_ref[...] = carry
        acc_ref[...] = acc_ref[...] + pv

    @pl.when(j == 0)
    def _():
        q = q_ref[...].astype(F32)
        qh_ref[...] = jnp.concatenate(
            [q[:, h * SB_HEAD_DIM:(h + 1) * SB_HEAD_DIM] for h in range(SB_HEADS)], axis=0).astype(BF16)
        carry_ref[...] = jnp.zeros(carry_ref.shape, F32)
        acc_ref[...] = jnp.zeros(acc_ref.shape, F32)
        row = lax.broadcasted_iota(I32, (rows, LANES), 0) % QPAD
        col = lax.broadcasted_iota(I32, (rows, LANES), 1)
        strict = (col < row) & (row < n_new)
        head = lambda ref: (lambda h: ref[:, h * SB_HEAD_DIM:(h + 1) * SB_HEAD_DIM])
        attend(head(kn_ref), head(vn_ref), strict)

    gather = lambda page_refs: (lambda h: jnp.concatenate(
        [_head_rows(r, h, SB_HEADS) for r in reversed(page_refs)], axis=0))
    attend(gather(k_refs), gather(v_refs), None)

    @pl.when(j == pl.num_programs(1) - 1)
    def _():
        for h in range(SB_HEADS):
            o_ref[:, h * SB_HEAD_DIM:(h + 1) * SB_HEAD_DIM] = acc_ref[h * QPAD:(h + 1) * QPAD, :]


def sb_sample_attention(page_table, q8, cache_k, cache_v, layer, k_new_pad, v_new_pad, n_new):
    nb, n_pages = page_table.shape
    assert n_pages % ATTN_PAGES == 0
    width = SB_HEADS * SB_HEAD_DIM
    per_b = lambda shape: pl.BlockSpec((None,) + shape, lambda b, j, pt: (b,) + (0,) * len(shape))
    order = lambda j, r: n_pages - 1 - (j * ATTN_PAGES + r)
    pages = lambda: _page_specs(ATTN_PAGES, PAGE * SB_HEADS, SB_HEAD_DIM, layer, order)
    grid_spec = pltpu.PrefetchScalarGridSpec(
        num_scalar_prefetch=1,
        grid=(nb, n_pages // ATTN_PAGES),
        in_specs=[per_b((QPAD, width))] + pages() + pages() + [per_b((PAGE, width)), per_b((PAGE, width))],
        out_specs=per_b((QPAD, width)),
        scratch_shapes=[pltpu.VMEM((SB_HEADS * QPAD, SB_HEAD_DIM), BF16),
                        pltpu.VMEM((SB_HEADS * QPAD, LANES), F32),
                        pltpu.VMEM((SB_HEADS * QPAD, SB_HEAD_DIM), F32)],
    )
    return pl.pallas_call(
        functools.partial(_sb_sample_kernel, n_new=n_new),
        grid_spec=grid_spec,
        out_shape=jax.ShapeDtypeStruct((nb, QPAD, width), F32),
        compiler_params=_cparams(("parallel", "arbitrary"), VMEM_LIMIT),
        name="sb_sample_attention",
    )(page_table, q8, *([cache_k] * ATTN_PAGES), *([cache_v] * ATTN_PAGES), k_new_pad, v_new_pad)


def _pad_rows(a, nb, n_new, rows):
    a = a.reshape(nb, n_new, a.shape[-1])
    return jnp.pad(a, ((0, 0), (0, rows - n_new), (0, 0)))


def _token_head_rows(cache):
    return cache.reshape(cache.shape[:2] + (cache.shape[2] * cache.shape[3], cache.shape[4]))


def dsa_sample(page_table, q, qi, wi, k_new, v_new, ki_new, cache_k, cache_v, cache_idx_k, layer, topk):
    nb = page_table.shape[0]
    n_new = q.shape[0] // nb
    sel = dsa_sample_select(page_table, _pad_rows(qi, nb, n_new, QPAD), _pad_rows(wi, nb, n_new, QPAD),
                            jnp.swapaxes(cache_idx_k, 2, 3), layer,
                            jnp.swapaxes(_pad_rows(ki_new, nb, n_new, PAGE), 1, 2), n_new, topk)
    o = dsa_sample_attention(page_table, _pad_rows(q, nb, n_new, QPAD), sel,
                             _token_head_rows(cache_k), _token_head_rows(cache_v), layer,
                             _pad_rows(k_new, nb, n_new, PAGE), _pad_rows(v_new, nb, n_new, PAGE))
    return o[:, :n_new].reshape(nb * n_new, -1)


def sb_sample(page_table, q, k_new, v_new, cache_k, cache_v, layer):
    nb = page_table.shape[0]
    n_new = q.shape[0] // nb
    o = sb_sample_attention(page_table, _pad_rows(q, nb, n_new, QPAD),
                            _token_head_rows(cache_k), _token_head_rows(cache_v), layer,
                            _pad_rows(k_new, nb, n_new, PAGE), _pad_rows(v_new, nb, n_new, PAGE), n_new)
    return o[:, :n_new].reshape(nb * n_new, -1)


S5_TILE_U = 128
S5_TILE_S = S5_TILE_U // S5_GROUP * S5_STATE
S5_NTILE = D_MODEL // S5_TILE_U
S5_TILES_PER_HALF = S5_NTILE // 2
S5_LANE_TILES = S5_HALF // LANES


def _s5_kernel(x_ref, g_ref, d_ref, bx_ref, ct_ref, are_ref, aim_ref, h0re_ref, h0im_ref,
               y_ref, hre_out, him_out, xre, xim, hre, him, *, nb, tc, t_last, split_input):
    c = pl.program_id(0)
    rows_half = (nb // 2) * tc

    @pl.when(c == 0)
    def _():
        hre[...] = h0re_ref[...]
        him[...] = h0im_ref[...]

    u = _rms(x_ref[...].reshape(rows_half, D_MODEL), g_ref[...])
    u_hi, u_lo = _split_bf16(u)
    lanes_per_tile = S5_TILE_S // LANES
    for tile in range(S5_NTILE):
        half, ctile = divmod(tile, S5_TILES_PER_HALF)
        rows = slice(half * rows_half, (half + 1) * rows_half)
        ucol = slice(tile * S5_TILE_U, (tile + 1) * S5_TILE_U)
        a_hi, a_lo = u_hi[:, ucol], u_lo[:, ucol]
        re = _dot(a_hi, bx_ref[0, tile])
        im = _dot(a_hi, bx_ref[2, tile])
        if split_input:
            re = re + _dot(a_lo, bx_ref[0, tile]) + _dot(a_hi, bx_ref[1, tile])
            im = im + _dot(a_lo, bx_ref[2, tile]) + _dot(a_hi, bx_ref[3, tile])
        for k in range(lanes_per_tile):
            lt = ctile * lanes_per_tile + k
            xre[lt, rows, :] = re[:, k * LANES:(k + 1) * LANES]
            xim[lt, rows, :] = im[:, k * LANES:(k + 1) * LANES]

    def step(t, carry):
        idx = pl.ds(t, nb, stride=tc)
        new = []
        for lt in range(S5_LANE_TILES):
            hr, hi = carry[lt]
            ar = are_ref[lt]
            ai = aim_ref[lt]
            nr = ar * hr - ai * hi + xre[lt, idx, :]
            ni = ar * hi + ai * hr + xim[lt, idx, :]
            xre[lt, idx, :] = nr
            xim[lt, idx, :] = ni
            new.append((nr, ni))
        return tuple(new)

    h_end = lax.fori_loop(0, tc, step, tuple((hre[lt], him[lt]) for lt in range(S5_LANE_TILES)))
    for lt in range(S5_LANE_TILES):
        hre[lt] = h_end[lt][0]
        him[lt] = h_end[lt][1]

    ys = []
    for tile in range(S5_NTILE):
        half, ctile = divmod(tile, S5_TILES_PER_HALF)
        rows = slice(half * rows_half, (half + 1) * rows_half)
        lts = range(ctile * lanes_per_tile, (ctile + 1) * lanes_per_tile)
        h_re = jnp.concatenate([xre[lt, rows, :] for lt in lts], axis=1).astype(BF16)
        h_im = jnp.concatenate([xim[lt, rows, :] for lt in lts], axis=1).astype(BF16)
        ys.append(_dot(h_re, ct_ref[0, tile]) + _dot(h_im, ct_ref[1, tile]))
    y = jnp.concatenate(ys, axis=1) + d_ref[...] * u
    y_ref[...] = y.reshape(y_ref.shape)

    @pl.when(c == t_last // tc)
    def _():
        idx = pl.ds(t_last % tc, nb, stride=tc)
        for lt in range(S5_LANE_TILES):
            hre_out[lt] = xre[lt, idx, :]
            him_out[lt] = xim[lt, idx, :]


def _s5_discretize(log_dt, a_re, a_im, b_re, b_im, c_re, c_im):
    dt = jnp.exp(log_dt)[:, None]
    decay = jnp.exp(dt * a_re)
    ang = dt * a_im
    ab_re = decay * jnp.cos(ang)
    ab_im = decay * jnp.sin(ang)
    den = a_re * a_re + a_im * a_im
    nr = ab_re - 1.0
    cf_re = (nr * a_re + ab_im * a_im) / den
    cf_im = (ab_im * a_re - nr * a_im) / den
    bx_re = cf_re[:, :, None] * b_re - cf_im[:, :, None] * b_im
    bx_im = cf_re[:, :, None] * b_im + cf_im[:, :, None] * b_re
    gpt = S5_TILE_U // S5_GROUP
    eye = jnp.eye(gpt, dtype=F32)

    def b_tiles(b):
        bt = b.reshape(S5_NTILE, gpt, S5_STATE, S5_GROUP)
        return jnp.einsum('tgpc,gh->tgchp', bt, eye).reshape(S5_NTILE, S5_TILE_U, S5_TILE_S)

    def c_tiles(cm):
        ctl = cm.reshape(S5_NTILE, gpt, S5_GROUP, S5_STATE)
        return jnp.einsum('tgcp,gh->tgphc', ctl, eye).reshape(S5_NTILE, S5_TILE_S, S5_TILE_U)

    bre_hi, bre_lo = _split_bf16(b_tiles(bx_re))
    bim_hi, bim_lo = _split_bf16(b_tiles(bx_im))
    bx = jnp.stack([bre_hi, bre_lo, bim_hi, bim_lo])
    ct = jnp.stack([c_tiles(c_re), -c_tiles(c_im)]).astype(BF16)
    return ab_re.reshape(-1), ab_im.reshape(-1), bx, ct


def _s5_rows(vec, n_seq):
    v = vec.reshape(2, 1, S5_LANE_TILES, LANES)
    return jnp.broadcast_to(v, (2, n_seq, S5_LANE_TILES, LANES)).transpose(2, 0, 1, 3).reshape(
        S5_LANE_TILES, 2 * n_seq, LANES)


def _s5_state_to_rows(h):
    n = h.shape[0]
    return h.reshape(n, 2, S5_LANE_TILES, LANES).transpose(2, 1, 0, 3).reshape(S5_LANE_TILES, 2 * n, LANES)


def _s5_rows_to_state(r):
    n = r.shape[1] // 2
    return r.reshape(S5_LANE_TILES, 2, n, LANES).transpose(2, 1, 0, 3).reshape(n, S5_GROUPS, S5_STATE)


def _s5_chunk_len(t_total):
    best = None
    for c in range(SUBLANES, QBLK + 1, 2 * SUBLANES):
        if t_total % c == 0:
            best = c
    assert best is not None
    return best


def s5_scan(x3, g, d_skip, disc, h0_re, h0_im, n_seq, tc, t_last, split_input):
    ab_re, ab_im, bx, ct = disc
    gdim, t_total, _ = x3.shape
    rchunk = n_seq * tc // gdim
    assert t_total % rchunk == 0
    nb = 2 * n_seq
    nchunks = t_total // rchunk
    st = (S5_LANE_TILES, nb, LANES)
    buf = (S5_LANE_TILES, nb * tc, LANES)
    y, hre, him = pl.pallas_call(
        functools.partial(_s5_kernel, nb=nb, tc=tc, t_last=t_last, split_input=split_input),
        grid=(nchunks,),
        in_specs=[pl.BlockSpec((gdim, rchunk, D_MODEL), lambda c: (0, c, 0)),
                  _const_spec((1, D_MODEL)), _const_spec((1, D_MODEL)),
                  _const_spec(bx.shape), _const_spec(ct.shape),
                  _const_spec(st), _const_spec(st), _const_spec(st), _const_spec(st)],
        out_specs=[pl.BlockSpec((gdim, rchunk, D_MODEL), lambda c: (0, c, 0)),
                   pl.BlockSpec(st, lambda c: (0, 0, 0)), pl.BlockSpec(st, lambda c: (0, 0, 0))],
        out_shape=[jax.ShapeDtypeStruct(x3.shape, F32),
                   jax.ShapeDtypeStruct(st, F32), jax.ShapeDtypeStruct(st, F32)],
        scratch_shapes=[pltpu.VMEM(buf, F32), pltpu.VMEM(buf, F32),
                        pltpu.VMEM(st, F32), pltpu.VMEM(st, F32)],
        compiler_params=_cparams(("arbitrary",), VMEM_LIMIT),
        name="s5_scan",
    )(x3, g.reshape(1, D_MODEL), d_skip.reshape(1, D_MODEL), bx, ct,
      _s5_rows(ab_re, n_seq), _s5_rows(ab_im, n_seq), _s5_state_to_rows(h0_re), _s5_state_to_rows(h0_im))
    return y, _s5_rows_to_state(hre), _s5_rows_to_state(him)


def kernel(x_prompt, x_sample, cache_dsa_k, cache_dsa_v, cache_dsa_idx_k, state_ssm_re, state_ssm_im,
           cache_sb_k, cache_sb_v, page_table, meta_tokens, g_mix, g_ffn,
           dsa_w_in, dsa_q_norm, dsa_k_norm, dsa_w_out,
           s5_log_dt, s5_a_re, s5_a_im, s5_b_re, s5_b_im, s5_c_re, s5_c_im, s5_d, s5_w_out,
           sb_w_in, sb_w_out, ffn_w_gu, ffn_w_down, moe_w_router, moe_b_router, moe_w_gu, moe_w_down):
    bsz, seq, _ = x_prompt.shape
    t_p = seq + N_META
    t_pad = -(-t_p // KEY_CHUNK) * KEY_CHUNK
    meta = jnp.broadcast_to(meta_tokens[None], (bsz, N_META, D_MODEL))
    xp = jnp.concatenate([meta, x_prompt, jnp.zeros((bsz, t_pad - t_p, D_MODEL), F32)], axis=1)
    xp = xp.reshape(bsz * t_pad, D_MODEL)
    nb, n_new, _ = x_sample.shape
    xs = x_sample.reshape(nb * n_new, D_MODEL)
    past_len = page_table.shape[1] * PAGE
    topk_p = min(TOPK_MAX, seq // 4)
    topk_s = min(TOPK_MAX, (past_len + n_new) // 4)
    tm_p = PROMPT_TILE
    tm_s = nb * n_new

    p3 = lambda a: a.reshape(bsz, t_pad, a.shape[-1])
    out_p = lambda a, *hd: a.reshape((bsz, t_pad) + hd)[:, :t_p]
    out_s = lambda a, *hd: a.reshape((nb, n_new) + hd)

    outs = {name: [] for name in ("dsa_k_p", "dsa_k_s", "dsa_v_p", "dsa_v_s", "dsa_i_p", "dsa_i_s",
                                  "ssm_re_p", "ssm_re_s", "ssm_im_p", "ssm_im_s",
                                  "sb_k_p", "sb_k_s", "sb_v_p", "sb_v_s")}
    ia = ib = ic = i_dense = i_moe = 0
    depth = g_mix.shape[0]
    for layer in range(depth):
        kind = layer % 3
        if kind == 0:
            w_in = jnp.concatenate(
                [dsa_w_in[ia], jnp.zeros((D_MODEL, DSA_IN_PAD - dsa_w_in.shape[2]), F32)], axis=1).astype(BF16)
            w_out = dsa_w_out[ia].astype(BF16)
            q, k, v, qi, ki, wi = dsa_project(xp, g_mix[layer], w_in, dsa_q_norm[ia], dsa_k_norm[ia], tm_p)
            o = dsa_attention_prompt(p3(q), p3(qi), p3(wi), p3(k), p3(v), p3(ki), topk_p)
            xp = out_project(xp, o.reshape(bsz * t_pad, D_MODEL), w_out, tm_p)
            qs, ks, vs, qis, kis, wis = dsa_project(xs, g_mix[layer], w_in, dsa_q_norm[ia], dsa_k_norm[ia], tm_s)
            o_s = dsa_sample(page_table, qs, qis, wis, ks, vs, kis,
                             cache_dsa_k, cache_dsa_v, cache_dsa_idx_k, ia, topk_s)
            xs = out_project(xs, o_s.astype(BF16), w_out, tm_s)
            outs["dsa_k_p"].append(out_p(k, DSA_KV_HEADS, HEAD_DIM))
            outs["dsa_v_p"].append(out_p(v, DSA_KV_HEADS, HEAD_DIM))
            outs["dsa_i_p"].append(out_p(ki, IDX_DIM))
            outs["dsa_k_s"].append(out_s(ks, DSA_KV_HEADS, HEAD_DIM))
            outs["dsa_v_s"].append(out_s(vs, DSA_KV_HEADS, HEAD_DIM))
            outs["dsa_i_s"].append(out_s(kis, IDX_DIM))
            ia += 1
        elif kind == 1:
            disc = _s5_discretize(s5_log_dt[ib], s5_a_re[ib], s5_a_im[ib], s5_b_re[ib], s5_b_im[ib],
                                  s5_c_re[ib], s5_c_im[ib])
            w_out = s5_w_out[ib].astype(BF16)
            zeros = jnp.zeros((bsz, S5_GROUPS, S5_STATE), F32)
            yp, hre_p, him_p = s5_scan(p3(xp), g_mix[layer], s5_d[ib], disc, zeros, zeros, bsz,
                                       _s5_chunk_len(t_pad), t_p - 1, False)
            xp = s5_glu(xp, yp.reshape(bsz * t_pad, D_MODEL), w_out, tm_p)
            ys, hre_s, him_s = s5_scan(xs.reshape(1, nb * n_new, D_MODEL), g_mix[layer], s5_d[ib], disc,
                                       state_ssm_re[ib], state_ssm_im[ib], nb, n_new, n_new - 1, True)
            xs = s5_glu(xs, ys.reshape(nb * n_new, D_MODEL), w_out, tm_s)
            outs["ssm_re_p"].append(hre_p)
            outs["ssm_im_p"].append(him_p)
            outs["ssm_re_s"].append(hre_s)
            outs["ssm_im_s"].append(him_s)
            ib += 1
        else:
            w_in = sb_w_in[ic].astype(BF16)
            w_out = sb_w_out[ic].astype(BF16)
            q, k, v, kb, vb = sb_project(xp, g_mix[layer], w_in, tm_p)
            o = sb_attention_prompt(p3(q), p3(kb), p3(vb))
            xp = out_project(xp, o.reshape(bsz * t_pad, D_MODEL), w_out, tm_p)
            qs, ks, vs, _, _ = sb_project(xs, g_mix[layer], w_in, tm_s)
            o_s = sb_sample(page_table, qs, ks, vs, cache_sb_k, cache_sb_v, ic)
            xs = out_project(xs, o_s.astype(BF16), w_out, tm_s)
            outs["sb_k_p"].append(out_p(k, SB_HEADS, SB_HEAD_DIM))
            outs["sb_v_p"].append(out_p(v, SB_HEADS, SB_HEAD_DIM))
            outs["sb_k_s"].append(out_s(ks, SB_HEADS, SB_HEAD_DIM))
            outs["sb_v_s"].append(out_s(vs, SB_HEADS, SB_HEAD_DIM))
            ic += 1
        if layer % 2 == 0:
            wgu = ffn_w_gu[i_dense].astype(BF16)
            wd = ffn_w_down[i_dense].astype(BF16)
            xp = ffn(xp, g_ffn[layer], wgu, wd, tm_p)
            xs = ffn(xs, g_ffn[layer], wgu, wd, tm_s)
            i_dense += 1
        else:
            wr = jnp.concatenate([moe_w_router[i_moe], jnp.zeros((D_MODEL, LANES - N_EXPERTS), F32)], axis=1)
            br = jnp.concatenate([moe_b_router[i_moe], jnp.full((LANES - N_EXPERTS,), NEG_BIG, F32)])
            br = br.reshape(1, LANES)
            wgu = moe_w_gu[i_moe].astype(BF16)
            wd = moe_w_down[i_moe].astype(BF16)
            xp = moe_routed(xp, g_ffn[layer], wr, br, wgu, wd, MOE_TILE)
            xs = moe(xs, g_ffn[layer], wr, br, wgu, wd, tm_s)
            i_moe += 1

    y_prompt = xp.reshape(bsz, t_pad, D_MODEL)[:, N_META:t_p]
    y_sample = xs.reshape(nb, n_new, D_MODEL)
    stack = lambda name: jnp.stack(outs[name])
    return (y_prompt, y_sample,
            stack("dsa_k_p"), stack("dsa_k_s"), stack("dsa_v_p"), stack("dsa_v_s"),
            stack("dsa_i_p"), stack("dsa_i_s"),
            stack("ssm_re_p"), stack("ssm_re_s"), stack("ssm_im_p"), stack("ssm_im_s"),
            stack("sb_k_p"), stack("sb_k_s"), stack("sb_v_p"), stack("sb_v_s"))
```

```python
import functools
import math

import jax
import jax.numpy as jnp
from jax import lax
from jax.experimental import pallas as pl
from jax.experimental.pallas import tpu as pltpu

F32 = jnp.float32
BF16 = jnp.bfloat16
I32 = jnp.int32

D_MODEL = 1024
N_META = 16
EPS = 1e-6
DSA_HEADS = 8
DSA_KV_HEADS = 2
DSA_GROUP = DSA_HEADS // DSA_KV_HEADS
HEAD_DIM = 128
IDX_HEADS = 8
IDX_DIM = 64
TOPK_MAX = 256
S5_GROUP = 16
S5_GROUPS = 64
S5_STATE = 64
S5_NSTATE = S5_GROUPS * S5_STATE
S5_HALF = S5_NSTATE // 2
SB_HEADS = 8
SB_HEAD_DIM = 128
FF_DIM = 11 * D_MODEL // 4
N_EXPERTS = 8
PAGE = 128

LANES = 128
SUBLANES = 8
VMEM_LIMIT = 56 * 1024 * 1024

QBLK = 128
KEY_SUB = 3
KEY_CHUNK = KEY_SUB * QBLK
PROMPT_TILE = 512
DSA_IN_DIM = (DSA_HEADS + 2 * DSA_KV_HEADS) * HEAD_DIM + IDX_HEADS * IDX_DIM + IDX_DIM + IDX_HEADS
DSA_IN_PAD = -(-DSA_IN_DIM // LANES) * LANES
INT_MIN = -(2 ** 31)
INT_MAX = 2 ** 31 - 1
NEG_BIG = -1e30


def _cparams(sem, vmem=None):
    return pltpu.CompilerParams(dimension_semantics=sem, vmem_limit_bytes=vmem)


def _const_spec(shape):
    nd = len(shape)
    return pl.BlockSpec(shape, lambda *_: (0,) * nd, pipeline_mode=pl.Buffered(1))


def _rms(x, g):
    return x * lax.rsqrt(jnp.mean(x * x, axis=-1, keepdims=True) + EPS) * g


def _dot(a, b):
    return jnp.dot(a, b, preferred_element_type=F32)


def _dot_nt(a, b):
    return lax.dot_general(a, b, (((1,), (1,)), ((), ())), preferred_element_type=F32)


def _split_bf16(x):
    hi = x.astype(BF16)
    lo = (x - hi.astype(F32)).astype(BF16)
    return hi, lo


def _lane_tiles(x):
    return [x[:, u * LANES:(u + 1) * LANES] for u in range(x.shape[-1] // LANES)]


def _dsa_proj_kernel(x_ref, g_ref, w_ref, qg_ref, kg_ref,
                     q_ref, k_ref, v_ref, qi_ref, ki_ref, wi_ref, kb_ref, vb_ref):
    h = _rms(x_ref[...], g_ref[...]).astype(BF16)
    nq = DSA_HEADS * HEAD_DIM
    nkv = DSA_KV_HEADS * HEAD_DIM
    q = _dot(h, w_ref[:, 0:nq])
    scale = 1.0 / math.sqrt(HEAD_DIM)
    for hd in range(DSA_HEADS):
        sl = slice(hd * HEAD_DIM, (hd + 1) * HEAD_DIM)
        q_ref[:, sl] = (_rms(q[:, sl], qg_ref[...]) * scale).astype(BF16)
    k = _dot(h, w_ref[:, nq:nq + nkv])
    for hd in range(DSA_KV_HEADS):
        sl = slice(hd * HEAD_DIM, (hd + 1) * HEAD_DIM)
        kn = _rms(k[:, sl], kg_ref[...])
        k_ref[:, sl] = kn
        kb_ref[:, sl] = kn.astype(BF16)
    v = _dot(h, w_ref[:, nq + nkv:nq + 2 * nkv])
    v_ref[...] = v
    vb_ref[...] = v.astype(BF16)
    o = nq + 2 * nkv
    qi_ref[...] = _dot(h, w_ref[:, o:o + IDX_HEADS * IDX_DIM]).astype(BF16)
    o += IDX_HEADS * IDX_DIM
    tail = _dot(h, w_ref[:, o:o + LANES])
    ki_ref[...] = tail[:, 0:IDX_DIM]
    wi_ref[...] = tail[:, IDX_DIM:IDX_DIM + IDX_HEADS]


def dsa_project(x, g, w_pad, qg, kg, tm):
    m = x.shape[0]
    assert m % tm == 0
    nq = DSA_HEADS * HEAD_DIM
    nkv = DSA_KV_HEADS * HEAD_DIM
    row = lambda n: pl.BlockSpec((tm, n), lambda i: (i, 0))
    return pl.pallas_call(
        _dsa_proj_kernel,
        grid=(m // tm,),
        in_specs=[row(D_MODEL), _const_spec((1, D_MODEL)), _const_spec(w_pad.shape),
                  _const_spec((1, HEAD_DIM)), _const_spec((1, HEAD_DIM))],
        out_specs=[row(nq), row(nkv), row(nkv), row(IDX_HEADS * IDX_DIM), row(IDX_DIM), row(IDX_HEADS),
                   row(nkv), row(nkv)],
        out_shape=[jax.ShapeDtypeStruct((m, nq), BF16),
                   jax.ShapeDtypeStruct((m, nkv), F32),
                   jax.ShapeDtypeStruct((m, nkv), F32),
                   jax.ShapeDtypeStruct((m, IDX_HEADS * IDX_DIM), BF16),
                   jax.ShapeDtypeStruct((m, IDX_DIM), F32),
                   jax.ShapeDtypeStruct((m, IDX_HEADS), F32),
                   jax.ShapeDtypeStruct((m, nkv), BF16),
                   jax.ShapeDtypeStruct((m, nkv), BF16)],
        compiler_params=_cparams(("parallel",), VMEM_LIMIT),
        name="dsa_project",
    )(x, g.reshape(1, D_MODEL), w_pad, qg.reshape(1, HEAD_DIM), kg.reshape(1, HEAD_DIM))


def _sb_proj_kernel(x_ref, g_ref, w_ref, q_ref, k_ref, v_ref, kb_ref, vb_ref):
    h = _rms(x_ref[...], g_ref[...]).astype(BF16)
    scale = 1.0 / math.sqrt(SB_HEAD_DIM)
    q_ref[...] = (_dot(h, w_ref[:, 0:D_MODEL]) * scale).astype(BF16)
    k = _dot(h, w_ref[:, D_MODEL:2 * D_MODEL])
    k_ref[...] = k
    kb_ref[...] = k.astype(BF16)
    v = _dot(h, w_ref[:, 2 * D_MODEL:3 * D_MODEL])
    v_ref[...] = v
    vb_ref[...] = v.astype(BF16)


def sb_project(x, g, w, tm):
    m = x.shape[0]
    assert m % tm == 0
    row = lambda n: pl.BlockSpec((tm, n), lambda i: (i, 0))
    return pl.pallas_call(
        _sb_proj_kernel,
        grid=(m // tm,),
        in_specs=[row(D_MODEL), _const_spec((1, D_MODEL)), _const_spec(w.shape)],
        out_specs=[row(D_MODEL)] * 5,
        out_shape=[jax.ShapeDtypeStruct((m, D_MODEL), BF16),
                   jax.ShapeDtypeStruct((m, D_MODEL), F32),
                   jax.ShapeDtypeStruct((m, D_MODEL), F32),
                   jax.ShapeDtypeStruct((m, D_MODEL), BF16),
                   jax.ShapeDtypeStruct((m, D_MODEL), BF16)],
        compiler_params=_cparams(("parallel",), VMEM_LIMIT),
        name="sb_project",
    )(x, g.reshape(1, D_MODEL), w)


def _out_proj_kernel(x_ref, o_ref, w_ref, y_ref):
    y_ref[...] = x_ref[...] + _dot(o_ref[...], w_ref[...])


def out_project(x, o, w, tm):
    m = x.shape[0]
    assert m % tm == 0
    row = pl.BlockSpec((tm, D_MODEL), lambda i: (i, 0))
    return pl.pallas_call(
        _out_proj_kernel,
        grid=(m // tm,),
        in_specs=[row, row, _const_spec(w.shape)],
        out_specs=row,
        out_shape=jax.ShapeDtypeStruct((m, D_MODEL), F32),
        compiler_params=_cparams(("parallel",), VMEM_LIMIT),
        name="out_project",
    )(x, o, w)


def _gelu_tanh(y):
    c = math.sqrt(2.0 / math.pi)
    return 0.5 * y * (1.0 + jnp.tanh(c * (y + 0.044715 * (y * y * y))))


def _s5_glu_kernel(x_ref, y_ref, w_ref, o_ref):
    z = _gelu_tanh(y_ref[...]).astype(BF16)
    a = _dot(z, w_ref[:, 0:D_MODEL])
    g = _dot(z, w_ref[:, D_MODEL:2 * D_MODEL])
    o_ref[...] = x_ref[...] + a * jax.nn.sigmoid(g)


def s5_glu(x, y, w, tm):
    m = x.shape[0]
    assert m % tm == 0
    row = pl.BlockSpec((tm, D_MODEL), lambda i: (i, 0))
    return pl.pallas_call(
        _s5_glu_kernel,
        grid=(m // tm,),
        in_specs=[row, row, _const_spec(w.shape)],
        out_specs=row,
        out_shape=jax.ShapeDtypeStruct((m, D_MODEL), F32),
        compiler_params=_cparams(("parallel",), VMEM_LIMIT),
        name="s5_glu",
    )(x, y, w)


FF_CHUNK = 256


def _swiglu_acc(h, wgu_ref, wd_ref, acc):
    for c in range(FF_DIM // FF_CHUNK):
        lo = c * FF_CHUNK
        g = _dot(h, wgu_ref[:, lo:lo + FF_CHUNK])
        u = _dot(h, wgu_ref[:, FF_DIM + lo:FF_DIM + lo + FF_CHUNK])
        a = (g * jax.nn.sigmoid(g) * u).astype(BF16)
        acc = acc + _dot(a, wd_ref[lo:lo + FF_CHUNK, :])
    return acc


def _ffn_kernel(x_ref, g_ref, wgu_ref, wd_ref, o_ref):
    x = x_ref[...]
    h = _rms(x, g_ref[...]).astype(BF16)
    o_ref[...] = _swiglu_acc(h, wgu_ref, wd_ref, x)


def ffn(x, g, wgu, wd, tm):
    m = x.shape[0]
    assert m % tm == 0
    row = pl.BlockSpec((tm, D_MODEL), lambda i: (i, 0))
    return pl.pallas_call(
        _ffn_kernel,
        grid=(m // tm,),
        in_specs=[row, _const_spec((1, D_MODEL)), _const_spec(wgu.shape), _const_spec(wd.shape)],
        out_specs=row,
        out_shape=jax.ShapeDtypeStruct((m, D_MODEL), F32),
        compiler_params=_cparams(("parallel",), VMEM_LIMIT),
        name="ffn",
    )(x, g.reshape(1, D_MODEL), wgu, wd)


def _moe_kernel(x_ref, g_ref, wr_ref, br_ref, wgu_ref, wd_ref, o_ref, h_scr, comb_scr):
    e = pl.program_id(1)

    @pl.when(e == 0)
    def _():
        x = x_ref[...]
        hf = _rms(x, g_ref[...])
        h_scr[...] = hf.astype(BF16)
        h_hi, h_lo = _split_bf16(hf)
        w_hi, w_lo = _split_bf16(wr_ref[...])
        logits = _dot(h_hi, w_hi) + _dot(h_lo, w_hi) + _dot(h_hi, w_lo) + br_ref[...]
        lane = lax.broadcasted_iota(I32, logits.shape, 1)
        m1 = jnp.max(logits, axis=-1, keepdims=True)
        i1 = jnp.min(jnp.where(logits == m1, lane, LANES), axis=-1, keepdims=True)
        rest = jnp.where(lane == i1, NEG_BIG, logits)
        m2 = jnp.max(rest, axis=-1, keepdims=True)
        i2 = jnp.min(jnp.where(rest == m2, lane, LANES), axis=-1, keepdims=True)
        e2 = jnp.exp(m2 - m1)
        g1 = 1.0 / (1.0 + e2)
        g2 = e2 / (1.0 + e2)
        comb_scr[...] = jnp.where(lane == i1, g1, 0.0) + jnp.where(lane == i2, g2, 0.0)
        o_ref[...] = x

    lane = lax.broadcasted_iota(I32, comb_scr.shape, 1)
    gate = jnp.sum(jnp.where(lane == e, comb_scr[...], 0.0), axis=-1, keepdims=True)
    y = _swiglu_acc(h_scr[...], wgu_ref.at[0], wd_ref.at[0], jnp.zeros(o_ref.shape, F32))
    o_ref[...] = o_ref[...] + gate * y


def moe(x, g, w_router_pad, b_router_pad, wgu, wd, tm):
    m = x.shape[0]
    assert m % tm == 0
    row = pl.BlockSpec((tm, D_MODEL), lambda i, e: (i, 0))
    cst = lambda shape: pl.BlockSpec(shape, lambda i, e: (0,) * len(shape), pipeline_mode=pl.Buffered(1))
    return pl.pallas_call(
        _moe_kernel,
        grid=(m // tm, N_EXPERTS),
        in_specs=[row, cst((1, D_MODEL)), cst((D_MODEL, LANES)), cst((1, LANES)),
                  pl.BlockSpec((1, D_MODEL, 2 * FF_DIM), lambda i, e: (e, 0, 0)),
                  pl.BlockSpec((1, FF_DIM, D_MODEL), lambda i, e: (e, 0, 0))],
        out_specs=row,
        out_shape=jax.ShapeDtypeStruct((m, D_MODEL), F32),
        scratch_shapes=[pltpu.VMEM((tm, D_MODEL), BF16), pltpu.VMEM((tm, LANES), F32)],
        compiler_params=_cparams(("parallel", "arbitrary"), VMEM_LIMIT),
        name="moe",
    )(x, g.reshape(1, D_MODEL), w_router_pad, b_router_pad, wgu, wd)


MOE_TILE = 512
ROUTE_E1, ROUTE_E2, ROUTE_G1, ROUTE_G2 = 0, 1, 2, 3


def _route_kernel(x_ref, g_ref, wr_ref, br_ref, o_ref):
    hf = _rms(x_ref[...], g_ref[...])
    h_hi, h_lo = _split_bf16(hf)
    w_hi, w_lo = _split_bf16(wr_ref[...])
    logits = _dot(h_hi, w_hi) + _dot(h_lo, w_hi) + _dot(h_hi, w_lo) + br_ref[...]
    lane = lax.broadcasted_iota(I32, logits.shape, 1)
    m1 = jnp.max(logits, axis=-1, keepdims=True)
    i1 = jnp.min(jnp.where(logits == m1, lane, LANES), axis=-1, keepdims=True)
    rest = jnp.where(lane == i1, NEG_BIG, logits)
    m2 = jnp.max(rest, axis=-1, keepdims=True)
    i2 = jnp.min(jnp.where(rest == m2, lane, LANES), axis=-1, keepdims=True)
    e2 = jnp.exp(m2 - m1)
    g1 = 1.0 / (1.0 + e2)
    g2 = e2 / (1.0 + e2)
    o_ref[...] = (jnp.where(lane == ROUTE_E1, i1.astype(F32), 0.0) + jnp.where(lane == ROUTE_E2, i2.astype(F32), 0.0)
                  + jnp.where(lane == ROUTE_G1, g1, 0.0) + jnp.where(lane == ROUTE_G2, g2, 0.0))


def moe_route(x, g, w_router_pad, b_router_pad, tm):
    m = x.shape[0]
    assert m % tm == 0
    return pl.pallas_call(
        _route_kernel,
        grid=(m // tm,),
        in_specs=[pl.BlockSpec((tm, D_MODEL), lambda i: (i, 0)), _const_spec((1, D_MODEL)),
                  _const_spec((D_MODEL, LANES)), _const_spec((1, LANES))],
        out_specs=pl.BlockSpec((tm, LANES), lambda i: (i, 0)),
        out_shape=jax.ShapeDtypeStruct((m, LANES), F32),
        compiler_params=_cparams(("parallel",), VMEM_LIMIT),
        name="moe_route",
    )(x, g.reshape(1, D_MODEL), w_router_pad, b_router_pad)


def _row_gather_start(idx_ref, n_rows, src_hbm, dst, sem):
    def body(r, _):
        pltpu.make_async_copy(src_hbm.at[pl.ds(idx_ref[0, r], 1)], dst.at[pl.ds(r, 1)], sem).start()
        return 0
    lax.fori_loop(0, n_rows, body, 0, unroll=8)


def _row_gather_wait(n_rows, src_hbm, dst, sem):
    pltpu.make_async_copy(src_hbm.at[pl.ds(0, n_rows)], dst, sem).wait()


def _moe_expert_kernel(te_ref, nu_ref, src_cur, src_next, x_hbm, g_ref, wgu_ref, wd_ref, o_ref, xbuf, sem):
    i = pl.program_id(0)
    slot = i % 2
    n_used = nu_ref[0]
    tm = o_ref.shape[0]

    @pl.when(i == 0)
    def _():
        _row_gather_start(src_cur, tm, x_hbm, xbuf.at[0], sem.at[0])

    @pl.when(i + 1 < n_used)
    def _():
        _row_gather_start(src_next, tm, x_hbm, xbuf.at[1 - slot], sem.at[1 - slot])

    @pl.when(i < n_used)
    def _():
        _row_gather_wait(tm, x_hbm, xbuf.at[slot], sem.at[slot])
        h = _rms(xbuf[slot], g_ref[...]).astype(BF16)
        o_ref[...] = _swiglu_acc(h, wgu_ref.at[0], wd_ref.at[0], jnp.zeros(o_ref.shape, F32))

    @pl.when(i >= n_used)
    def _():
        o_ref[...] = jnp.zeros(o_ref.shape, F32)


def moe_experts(x, g, src_rows, tile_expert, n_used, wgu, wd, tm):
    n_tiles = src_rows.shape[0]
    cst = lambda shape: pl.BlockSpec(shape, lambda i, te, nu: (0,) * len(shape), pipeline_mode=pl.Buffered(1))
    smem_rows = lambda off: pl.BlockSpec((None, 1, tm), lambda i, te, nu: (jnp.minimum(i + off, n_tiles - 1), 0, 0),
                                         memory_space=pltpu.SMEM)
    grid_spec = pltpu.PrefetchScalarGridSpec(
        num_scalar_prefetch=2,
        grid=(n_tiles,),
        in_specs=[smem_rows(0), smem_rows(1), pl.BlockSpec(memory_space=pl.ANY), cst((1, D_MODEL)),
                  pl.BlockSpec((1, D_MODEL, 2 * FF_DIM), lambda i, te, nu: (te[i], 0, 0)),
                  pl.BlockSpec((1, FF_DIM, D_MODEL), lambda i, te, nu: (te[i], 0, 0))],
        out_specs=pl.BlockSpec((tm, D_MODEL), lambda i, te, nu: (i, 0)),
        scratch_shapes=[pltpu.VMEM((2, tm, D_MODEL), F32), pltpu.SemaphoreType.DMA((2,))],
    )
    return pl.pallas_call(
        _moe_expert_kernel,
        grid_spec=grid_spec,
        out_shape=jax.ShapeDtypeStruct((n_tiles * tm, D_MODEL), F32),
        compiler_params=_cparams(("arbitrary",), VMEM_LIMIT),
        name="moe_experts",
    )(tile_expert, n_used, src_rows, src_rows, x, g.reshape(1, D_MODEL), wgu, wd)


def _moe_combine_kernel(dst_cur, dst_next, x_ref, route_ref, y_hbm, o_ref, ybuf, sem):
    i = pl.program_id(0)
    slot = i % 2
    tm = o_ref.shape[0]
    rows = 2 * tm

    @pl.when(i == 0)
    def _():
        _row_gather_start(dst_cur, rows, y_hbm, ybuf.at[0], sem.at[0])

    @pl.when(i + 1 < pl.num_programs(0))
    def _():
        _row_gather_start(dst_next, rows, y_hbm, ybuf.at[1 - slot], sem.at[1 - slot])

    _row_gather_wait(rows, y_hbm, ybuf.at[slot], sem.at[slot])
    route = route_ref[...]
    g1 = route[:, ROUTE_G1:ROUTE_G1 + 1]
    g2 = route[:, ROUTE_G2:ROUTE_G2 + 1]
    o_ref[...] = x_ref[...] + g1 * ybuf[slot, 0:tm, :] + g2 * ybuf[slot, tm:rows, :]


def moe_combine(x, route, y_sorted, dest_rows, tm):
    m = x.shape[0]
    assert m % tm == 0
    n = m // tm
    smem_rows = lambda off: pl.BlockSpec((None, 1, 2 * tm), lambda i: (jnp.minimum(i + off, n - 1), 0, 0),
                                         memory_space=pltpu.SMEM)
    return pl.pallas_call(
        _moe_combine_kernel,
        grid=(n,),
        in_specs=[smem_rows(0), smem_rows(1), pl.BlockSpec((tm, D_MODEL), lambda i: (i, 0)),
                  pl.BlockSpec((tm, LANES), lambda i: (i, 0)), pl.BlockSpec(memory_space=pl.ANY)],
        out_specs=pl.BlockSpec((tm, D_MODEL), lambda i: (i, 0)),
        out_shape=jax.ShapeDtypeStruct((m, D_MODEL), F32),
        scratch_shapes=[pltpu.VMEM((2, 2 * tm, D_MODEL), F32), pltpu.SemaphoreType.DMA((2,))],
        compiler_params=_cparams(("arbitrary",), VMEM_LIMIT),
        name="moe_combine",
    )(dest_rows, dest_rows, x, route, y_sorted)


def moe_routed(x, g, w_router_pad, b_router_pad, wgu, wd, tm):
    m = x.shape[0]
    route = moe_route(x, g, w_router_pad, b_router_pad, tm)
    experts = route[:, :2].astype(I32)
    flat = experts.reshape(-1)
    onehot = (flat[:, None] == jnp.arange(N_EXPERTS, dtype=I32)[None, :]).astype(I32)
    csum = jnp.cumsum(onehot, axis=0)
    rank = jnp.take_along_axis(csum, flat[:, None], axis=1)[:, 0] - 1
    tiles_per_expert = (csum[-1] + tm - 1) // tm
    tile_end = jnp.cumsum(tiles_per_expert)
    row_start = (tile_end - tiles_per_expert) * tm
    dest = row_start[flat] + rank
    n_tiles = 2 * m // tm + N_EXPERTS
    src = jnp.zeros((n_tiles * tm,), I32).at[dest].set(jnp.arange(2 * m, dtype=I32) // 2, unique_indices=True)
    tile_ids = jnp.arange(n_tiles, dtype=I32)
    tile_expert = jnp.minimum(jnp.sum((tile_end[None, :] <= tile_ids[:, None]).astype(I32), axis=1), N_EXPERTS - 1)
    n_used = tile_end[-1:].astype(I32)
    y_sorted = moe_experts(x, g, src.reshape(n_tiles, 1, tm), tile_expert, n_used, wgu, wd, tm)
    dest_rows = dest.reshape(m // tm, tm, 2).transpose(0, 2, 1).reshape(m // tm, 1, 2 * tm)
    return moe_combine(x, route, y_sorted, dest_rows, tm)


NEG_INF_CODE = INT_MIN + 0x7FFFFF


def _code_to_float(code):
    return pltpu.bitcast(code ^ ((code >> 31) & 0x7FFFFFFF), F32)


def _kth_largest(count_ge, topk, shape):
    zero = jnp.zeros(shape, I32)
    c0 = jnp.where(count_ge(_code_to_float(zero)) >= topk, zero, jnp.full(shape, INT_MIN, I32))

    def body(k, code):
        cand = code | jnp.left_shift(jnp.int32(1), 30 - k)
        return jnp.where(count_ge(_code_to_float(cand)) >= topk, cand, code)

    code = lax.fori_loop(0, 31, body, c0)
    return _code_to_float(jnp.maximum(code, NEG_INF_CODE))


def _indexer_scores(lg, wib_ref, rows):
    tiles = []
    for u in range(lg.shape[1] // LANES):
        cols = slice(u * LANES, (u + 1) * LANES)
        su = wib_ref[0] * jnp.maximum(lg[0:rows, cols], 0.0)
        for h in range(1, IDX_HEADS):
            su = su + wib_ref[h] * jnp.maximum(lg[h * rows:(h + 1) * rows, cols], 0.0)
        tiles.append(su)
    return tiles


def _count_matrix():
    r = lax.broadcasted_iota(I32, (QBLK, 2 * QBLK), 0)
    c = lax.broadcasted_iota(I32, (QBLK, 2 * QBLK), 1)
    return jnp.where((r < c) | (c >= QBLK), 1.0, 0.0).astype(BF16)


def _softmax_update(groups, sels, m_ref, l_ref, acc_ref, rows):
    old = {hd: (m_ref[hd], l_ref[hd], acc_ref[hd]) for _, _, heads in groups for hd in heads}
    new = {}
    for s, v, heads in groups:
        ps = []
        m_news = []
        for j, hd in enumerate(heads):
            tiles = _lane_tiles(s[j * rows:(j + 1) * rows])
            mx = jnp.where(sels[0], tiles[0], NEG_BIG)
            for sel, tile in zip(sels[1:], tiles[1:]):
                mx = jnp.maximum(mx, jnp.where(sel, tile, NEG_BIG))
            m_new = jnp.maximum(old[hd][0], jnp.max(mx, axis=-1, keepdims=True))
            p = [jnp.where(sel, jnp.exp(tile - m_new), 0.0) for sel, tile in zip(sels, tiles)]
            ps.append(jnp.concatenate(p, axis=1).astype(BF16) if len(p) > 1 else p[0].astype(BF16))
            m_news.append(m_new)
        pv = _dot(jnp.concatenate(ps, axis=0), jnp.concatenate([v, jnp.ones(v.shape, BF16)], axis=1))
        for j, hd in enumerate(heads):
            m_old, l_old, acc_old = old[hd]
            blk = pv[j * rows:(j + 1) * rows]
            alpha = jnp.exp(m_old - m_news[j])
            new[hd] = (m_news[j], alpha * l_old + blk[:, LANES:], alpha * acc_old + blk[:, :LANES])
    for hd, (m_new, l_new, acc_new) in new.items():
        m_ref[hd] = m_new
        l_ref[hd] = l_new
        acc_ref[hd] = acc_new


REDUCE_ROWS = 4 * SUBLANES


def _group_reduce(x, op):
    rows, nq = x.shape
    part = op(x.reshape(rows // REDUCE_ROWS, REDUCE_ROWS, nq), axis=0)
    return op(part, axis=0, keepdims=True)


def _prefix_rows(x, carry):
    rows, nq = x.shape
    groups = rows // SUBLANES
    x3 = x.reshape(groups, SUBLANES, nq)
    r = lax.broadcasted_iota(I32, x3.shape, 1)
    s = x3
    for k in (1, 2, 4):
        s = s + jnp.where(r >= k, pltpu.roll(s, k, axis=1), 0.0)
    earlier = [None] * groups
    for v in range(groups):
        earlier[v] = carry
        carry = carry + s[v, SUBLANES - 1:SUBLANES, :]
    return (s - x3 + jnp.stack(earlier, axis=0)).reshape(rows, nq), carry


def _dsa_prompt_kernel(q_ref, qi_ref, wi_ref, k_ref, vt_ref, ki_ref, o_ref,
                       score_ref, m_ref, l_ref, acc_ref, *, topk):
    i = pl.program_id(1)
    nsup = i // KEY_SUB + 1
    shape = (KEY_CHUNK, QBLK)
    key_iota = lax.broadcasted_iota(I32, shape, 0)
    q_pos = i * QBLK + lax.broadcasted_iota(I32, shape, 1)
    row = (1, QBLK)

    qi = qi_ref[...].astype(F32)
    qih_t = jnp.concatenate([qi[:, h * IDX_DIM:(h + 1) * IDX_DIM].T for h in range(IDX_HEADS)],
                            axis=1).astype(BF16)
    wi_t = wi_ref[...].T

    def score_super(sc, _):
        start = pl.multiple_of(sc * KEY_CHUNK, KEY_CHUNK)
        lg = _dot(ki_ref[pl.ds(start, KEY_CHUNK), :].astype(BF16), qih_t)
        s = wi_t[0:1, :] * jnp.maximum(lg[:, 0:QBLK], 0.0)
        for h in range(1, IDX_HEADS):
            s = s + wi_t[h:h + 1, :] * jnp.maximum(lg[:, h * QBLK:(h + 1) * QBLK], 0.0)
        score_ref[sc] = jnp.where(start + key_iota <= q_pos, s, -jnp.inf)
        return 0

    lax.fori_loop(0, nsup, score_super, 0)

    def count(hit_fn):
        def body(sc, acc):
            hit = jnp.where(hit_fn(score_ref[sc]), 1.0, 0.0)
            return acc + jnp.sum(hit.reshape(KEY_CHUNK // REDUCE_ROWS, REDUCE_ROWS, QBLK), axis=0)
        acc = lax.fori_loop(0, nsup, body, jnp.zeros((REDUCE_ROWS, QBLK), F32))
        return jnp.sum(acc, axis=0, keepdims=True)

    def tie_cutoff(t, need):
        def body(sc, state):
            carry, cut = state
            eq = score_ref[sc] == t
            before, carry = _prefix_rows(jnp.where(eq, 1.0, 0.0), carry)
            pos = jnp.where(eq & (before < need), sc * KEY_CHUNK + key_iota, -1)
            return carry, jnp.maximum(cut, _group_reduce(pos, jnp.max))
        _, cut = lax.fori_loop(0, nsup, body, (jnp.zeros(row, F32), jnp.full(row, -1, I32)))
        return cut

    def threshold():
        t = _kth_largest(lambda x: count(lambda s: s >= x), float(topk), row)
        need = float(topk) - count(lambda s: s > t)
        ties = count(lambda s: s == t)
        partial = jnp.max(jnp.where(ties > need, 1.0, 0.0)) > 0.0
        return t, lax.cond(partial, lambda: tie_cutoff(t, need), lambda: jnp.full(row, INT_MAX, I32))

    t, cut = lax.cond((i + 1) * QBLK <= topk,
                      lambda: (jnp.full(row, -jnp.inf, F32), jnp.full(row, INT_MAX, I32)), threshold)

    q = q_ref[...].astype(F32)
    hd = lambda h: slice(h * HEAD_DIM, (h + 1) * HEAD_DIM)
    qts = [jnp.concatenate([q[:, hd(n * DSA_GROUP + g)].T for g in range(DSA_GROUP)], axis=1).astype(BF16)
           for n in range(DSA_KV_HEADS)]
    m_ref[...] = jnp.full(m_ref.shape, NEG_BIG, F32)
    l_ref[...] = jnp.zeros(l_ref.shape, F32)
    acc_ref[...] = jnp.zeros(acc_ref.shape, F32)
    ones_rows = jnp.ones((SUBLANES, KEY_CHUNK), BF16)

    def attend(sc, diagonal):
        start = pl.multiple_of(sc * KEY_CHUNK, KEY_CHUNK)
        s_idx = score_ref[sc]
        key_pos = start + key_iota
        sel = (s_idx > t) | ((s_idx == t) & (key_pos <= cut))
        if diagonal:
            sel = sel & (key_pos <= q_pos)
        kc = k_ref[pl.ds(start, KEY_CHUNK), :]
        vt = vt_ref[sc]
        old = [(m_ref[n], l_ref[n], acc_ref[n]) for n in range(DSA_KV_HEADS)]
        new = []
        for n in range(DSA_KV_HEADS):
            m_old, l_old, acc_old = old[n]
            ps, m_news = [], []
            for g in range(DSA_GROUP):
                st = _dot(kc[:, hd(n)], qts[n][:, g * QBLK:(g + 1) * QBLK])
                tile = jnp.where(sel, st, NEG_BIG)
                m_new = jnp.maximum(m_old[:, g * QBLK:(g + 1) * QBLK], _group_reduce(tile, jnp.max))
                ps.append(jnp.exp(tile - m_new).astype(BF16))
                m_news.append(m_new)
            m_new = jnp.concatenate(m_news, axis=1)
            alpha = jnp.exp(m_old - m_new)
            pv = _dot(jnp.concatenate([vt[hd(n), :], ones_rows], axis=0), jnp.concatenate(ps, axis=1))
            new.append((m_new, alpha * l_old + pv[HEAD_DIM:HEAD_DIM + 1], alpha * acc_old + pv[:HEAD_DIM]))
        for n in range(DSA_KV_HEADS):
            m_ref[n], l_ref[n], acc_ref[n] = new[n]

    def body(sc, _):
        attend(sc, False)
        return 0

    lax.fori_loop(0, nsup - 1, body, 0)
    attend(nsup - 1, True)
    for n in range(DSA_KV_HEADS):
        o_t = acc_ref[n] / l_ref[n]
        for g in range(DSA_GROUP):
            o_ref[:, hd(n * DSA_GROUP + g)] = o_t[:, g * QBLK:(g + 1) * QBLK].T.astype(BF16)


def dsa_attention_prompt(q, qi, wi, k, v, ki, topk):
    b, t, _ = q.shape
    assert t % KEY_CHUNK == 0
    nsc = t // KEY_CHUNK
    vt = jnp.swapaxes(v.reshape(b, nsc, KEY_CHUNK, v.shape[2]), 2, 3)
    blk = lambda n: pl.BlockSpec((None, QBLK, n), lambda bi, i: (bi, i, 0))
    full = lambda n: pl.BlockSpec((None, t, n), lambda bi, i: (bi, 0, 0))
    width = DSA_GROUP * QBLK
    return pl.pallas_call(
        functools.partial(_dsa_prompt_kernel, topk=topk),
        grid=(b, t // QBLK),
        in_specs=[blk(q.shape[2]), blk(qi.shape[2]), blk(wi.shape[2]), full(k.shape[2]),
                  pl.BlockSpec((None, nsc, v.shape[2], KEY_CHUNK), lambda bi, i: (bi, 0, 0, 0)),
                  full(ki.shape[2])],
        out_specs=blk(q.shape[2]),
        out_shape=jax.ShapeDtypeStruct(q.shape, BF16),
        scratch_shapes=[pltpu.VMEM((nsc, KEY_CHUNK, QBLK), F32),
                        pltpu.VMEM((DSA_KV_HEADS, 1, width), F32),
                        pltpu.VMEM((DSA_KV_HEADS, 1, width), F32),
                        pltpu.VMEM((DSA_KV_HEADS, HEAD_DIM, width), F32)],
        compiler_params=_cparams(("parallel", "arbitrary"), VMEM_LIMIT),
        name="dsa_attention_prompt",
    )(q, qi, wi, k, vt, ki)


def _suffix_matrix():
    r = lax.broadcasted_iota(I32, (QBLK, 2 * QBLK), 0)
    c = lax.broadcasted_iota(I32, (QBLK, 2 * QBLK), 1)
    return jnp.where((r > c) | (c >= QBLK), 1.0, 0.0).astype(BF16)


def _sb_weights(z, carry, suffix_mat, strict):
    rows = z.shape[0]
    lk = -(jnp.maximum(z, 0.0) + jnp.log(1.0 + jnp.exp(-jnp.abs(z))))
    lkm = lk if strict is None else jnp.where(strict, lk, 0.0)
    hi, lo = _split_bf16(lkm)
    his, los = _lane_tiles(hi), _lane_tiles(lo)
    nsub = len(his)
    sufs = [None] * nsub
    for u in reversed(range(nsub)):
        r2 = _dot(jnp.concatenate([his[u], los[u]], axis=0), suffix_mat)
        ru = r2[:rows] + r2[rows:]
        sufs[u] = ru[:, :QBLK] + carry
        carry = carry + ru[:, QBLK:]
    suffix = jnp.concatenate(sufs, axis=1) if nsub > 1 else sufs[0]
    w = jnp.exp(z + lk + suffix)
    if strict is not None:
        w = jnp.where(strict, w, 0.0)
    return carry, w


SB_HEADS_PER_STEP = 4


def _sb_weights_t(zt, carry, strict):
    rows, nq = zt.shape
    groups = rows // SUBLANES
    lk = -(jnp.maximum(zt, 0.0) + jnp.log(1.0 + jnp.exp(-jnp.abs(zt))))
    lkm = lk if strict is None else jnp.where(strict, lk, 0.0)
    x3 = lkm.reshape(groups, SUBLANES, nq)
    r = lax.broadcasted_iota(I32, x3.shape, 1)
    s = x3
    for k in (1, 2, 4):
        s = s + jnp.where(r < SUBLANES - k, pltpu.roll(s, SUBLANES - k, axis=1), 0.0)
    later = [None] * groups
    for v in reversed(range(groups)):
        later[v] = carry
        carry = carry + s[v, 0:1, :]
    suffix = (s - x3 + jnp.stack(later, axis=0)).reshape(rows, nq)
    w = jnp.exp(zt + lk + suffix)
    if strict is not None:
        w = jnp.where(strict, w, 0.0)
    return carry, w


def _sb_prompt_kernel(q_ref, k_ref, vt_ref, o_ref, carry_ref, acc_ref):
    i = pl.program_id(2)
    top = i // KEY_SUB
    q = q_ref[...]
    heads = range(SB_HEADS_PER_STEP)
    hsl = lambda h: slice(h * SB_HEAD_DIM, (h + 1) * SB_HEAD_DIM)
    qts = [q[:, hsl(h)].astype(F32).T.astype(BF16) for h in heads]
    shape = (KEY_CHUNK, QBLK)
    key_iota = lax.broadcasted_iota(I32, shape, 0)
    q_pos = i * QBLK + lax.broadcasted_iota(I32, shape, 1)
    carry_ref[...] = jnp.zeros(carry_ref.shape, F32)
    acc_ref[...] = jnp.zeros(acc_ref.shape, F32)

    def run(sc, masked):
        start = pl.multiple_of(sc * KEY_CHUNK, KEY_CHUNK)
        kc = k_ref[pl.ds(start, KEY_CHUNK), :]
        vt = vt_ref[sc]
        strict = (start + key_iota < q_pos) if masked else None
        old = [(carry_ref[h], acc_ref[h]) for h in heads]
        new = []
        for h in heads:
            carry, w = _sb_weights_t(_dot(kc[:, hsl(h)], qts[h]), old[h][0], strict)
            new.append((carry, old[h][1] + _dot(vt[hsl(h), :], w.astype(BF16))))
        for h in heads:
            carry_ref[h] = new[h][0]
            acc_ref[h] = new[h][1]

    run(top, True)

    def body(j, _):
        run(top - 1 - j, False)
        return 0

    lax.fori_loop(0, top, body, 0)
    for h in heads:
        o_ref[:, hsl(h)] = acc_ref[h].T.astype(BF16)


def sb_attention_prompt(q, k, v):
    b, t, _ = q.shape
    assert t % KEY_CHUNK == 0
    nsc = t // KEY_CHUNK
    width = SB_HEADS_PER_STEP * SB_HEAD_DIM
    vt = jnp.swapaxes(v.reshape(b, nsc, KEY_CHUNK, v.shape[2]), 2, 3)
    blk = pl.BlockSpec((None, QBLK, width), lambda bi, h, i: (bi, i, h))
    full = pl.BlockSpec((None, t, width), lambda bi, h, i: (bi, 0, h))
    vt_spec = pl.BlockSpec((None, nsc, width, KEY_CHUNK), lambda bi, h, i: (bi, 0, h, 0))
    return pl.pallas_call(
        _sb_prompt_kernel,
        grid=(b, SB_HEADS // SB_HEADS_PER_STEP, t // QBLK),
        in_specs=[blk, full, vt_spec],
        out_specs=blk,
        out_shape=jax.ShapeDtypeStruct(q.shape, BF16),
        scratch_shapes=[pltpu.VMEM((SB_HEADS_PER_STEP, 1, QBLK), F32),
                        pltpu.VMEM((SB_HEADS_PER_STEP, SB_HEAD_DIM, QBLK), F32)],
        compiler_params=_cparams(("parallel", "parallel", "arbitrary"), VMEM_LIMIT),
        name="sb_attention_prompt",
    )(q, k, vt)


SELECT_PAGES = 16
ATTN_PAGES = 8
QPAD = SUBLANES


def _page_specs(n_slots, rows, width, layer, order):
    def spec(r):
        return pl.BlockSpec((None, None, rows, width),
                            lambda b, j, pt: (layer, pt[b, order(j, r)], 0, 0))
    return [spec(r) for r in range(n_slots)]


def _head_rows(page_ref, head, n_heads):
    return page_ref[pl.ds(head, PAGE, stride=n_heads), :]


def _dsa_sample_select_kernel(pt_ref, qi_ref, wi_ref, *refs, n_pages, n_new, topk):
    ki_refs = refs[:SELECT_PAGES]
    kin_ref, sel_ref, score_ref, wib_ref, qih_ref = refs[SELECT_PAGES:]
    j = pl.program_id(1)
    shape = (QPAD, LANES)

    @pl.when(j == 0)
    def _():
        qi = qi_ref[...].astype(F32)
        qih_ref[...] = jnp.concatenate(
            [qi[:, h * IDX_DIM:(h + 1) * IDX_DIM] for h in range(IDX_HEADS)], axis=0).astype(BF16)
        wi = wi_ref[...]
        for h in range(IDX_HEADS):
            wib_ref[h] = jnp.broadcast_to(wi[:, h:h + 1], shape)

    def scores(ki_t):
        lg = _dot(qih_ref[...], ki_t.astype(BF16))
        return _indexer_scores(lg, wib_ref, QPAD)

    for r, su in enumerate(scores(jnp.concatenate([r[...] for r in ki_refs], axis=1))):
        score_ref[j * SELECT_PAGES + r] = su

    @pl.when(j == pl.num_programs(1) - 1)
    def _():
        row = lax.broadcasted_iota(I32, shape, 0)
        col = lax.broadcasted_iota(I32, shape, 1)
        visible = (col <= row) & (row < n_new)
        score_ref[n_pages] = jnp.where(visible, scores(kin_ref[...])[0], -jnp.inf)
        nchunk = n_pages + 1

        def count(x, strict):
            accs = [jnp.zeros(shape, F32) for _ in range(4)]
            for c in range(nchunk):
                s = score_ref[c]
                accs[c % 4] = accs[c % 4] + jnp.where((s > x) if strict else (s >= x), 1.0, 0.0)
            return jnp.sum((accs[0] + accs[1]) + (accs[2] + accs[3]), axis=-1, keepdims=True)

        t = _kth_largest(lambda x: count(x, False), float(topk), shape)
        need = jnp.broadcast_to(float(topk) - count(t, True), shape)

        tiles = [score_ref[c] for c in range(nchunk)]
        eqf = [jnp.where(s == t, 1.0, 0.0) for s in tiles]
        eqf += [jnp.zeros(shape, F32)] * (-nchunk % 2)
        cnt = _dot(jnp.concatenate(eqf, axis=0).astype(BF16), _count_matrix())
        carry = jnp.zeros(shape, F32)
        for c in range(nchunk):
            cc = cnt[c * QPAD:(c + 1) * QPAD]
            sel = (tiles[c] > t) | ((tiles[c] == t) & (cc[:, :QBLK] + carry < need))
            if c == n_pages:
                sel = sel & visible
            sel_ref[c] = jnp.where(sel, 1.0, 0.0)
            carry = carry + cc[:, QBLK:]


def dsa_sample_select(page_table, qi8, wi8, cache_idx_k_t, layer, ki_new_t, n_new, topk):
    nb, n_pages = page_table.shape
    assert n_pages % SELECT_PAGES == 0
    per_b = lambda shape: pl.BlockSpec((None,) + shape, lambda b, j, pt: (b,) + (0,) * len(shape))
    grid_spec = pltpu.PrefetchScalarGridSpec(
        num_scalar_prefetch=1,
        grid=(nb, n_pages // SELECT_PAGES),
        in_specs=[per_b((QPAD, IDX_HEADS * IDX_DIM)), per_b((QPAD, IDX_HEADS))]
        + _page_specs(SELECT_PAGES, IDX_DIM, PAGE, layer, lambda j, r: j * SELECT_PAGES + r)
        + [per_b((IDX_DIM, PAGE))],
        out_specs=per_b((n_pages + 1, QPAD, LANES)),
        scratch_shapes=[pltpu.VMEM((n_pages + 1, QPAD, LANES), F32),
                        pltpu.VMEM((IDX_HEADS, QPAD, LANES), F32),
                        pltpu.VMEM((IDX_HEADS * QPAD, IDX_DIM), BF16)],
    )
    return pl.pallas_call(
        functools.partial(_dsa_sample_select_kernel, n_pages=n_pages, n_new=n_new, topk=topk),
        grid_spec=grid_spec,
        out_shape=jax.ShapeDtypeStruct((nb, n_pages + 1, QPAD, LANES), F32),
        compiler_params=_cparams(("parallel", "arbitrary"), VMEM_LIMIT),
        name="dsa_sample_select",
    )(page_table, qi8, wi8, *([cache_idx_k_t] * SELECT_PAGES), ki_new_t)


def _dsa_sample_attn_kernel(pt_ref, q_ref, sel_ref, *refs, n_pages):
    k_refs = refs[:ATTN_PAGES]
    v_refs = refs[ATTN_PAGES:2 * ATTN_PAGES]
    kn_ref, vn_ref, o_ref, qn_ref, m_ref, l_ref, acc_ref = refs[2 * ATTN_PAGES:]
    j = pl.program_id(1)

    @pl.when(j == 0)
    def _():
        q = q_ref[...].astype(F32)
        for n in range(DSA_KV_HEADS):
            qn_ref[n] = jnp.concatenate(
                [q[:, (n * DSA_GROUP + g) * HEAD_DIM:(n * DSA_GROUP + g + 1) * HEAD_DIM]
                 for g in range(DSA_GROUP)], axis=0).astype(BF16)
        m_ref[...] = jnp.full(m_ref.shape, NEG_BIG, F32)
        l_ref[...] = jnp.zeros(l_ref.shape, F32)
        acc_ref[...] = jnp.zeros(acc_ref.shape, F32)

    def attend(k_of, v_of, sels):
        groups = []
        for n in range(DSA_KV_HEADS):
            heads = [n * DSA_GROUP + g for g in range(DSA_GROUP)]
            groups.append((_dot_nt(qn_ref[n], k_of(n).astype(BF16)), v_of(n).astype(BF16), heads))
        _softmax_update(groups, sels, m_ref, l_ref, acc_ref, QPAD)

    gather = lambda page_refs: (lambda n: jnp.concatenate(
        [_head_rows(r, n, DSA_KV_HEADS) for r in page_refs], axis=0))
    attend(gather(k_refs), gather(v_refs), [sel_ref[j * ATTN_PAGES + r] > 0.5 for r in range(ATTN_PAGES)])

    @pl.when(j == pl.num_programs(1) - 1)
    def _():
        head = lambda ref: (lambda n: ref[:, n * HEAD_DIM:(n + 1) * HEAD_DIM])
        attend(head(kn_ref), head(vn_ref), [sel_ref[n_pages] > 0.5])
        for h in range(DSA_HEADS):
            o_ref[:, h * HEAD_DIM:(h + 1) * HEAD_DIM] = acc_ref[h] / l_ref[h]


def dsa_sample_attention(page_table, q8, sel, cache_k, cache_v, layer, k_new_pad, v_new_pad):
    nb, n_pages = page_table.shape
    assert n_pages % ATTN_PAGES == 0
    width = DSA_KV_HEADS * HEAD_DIM
    per_b = lambda shape: pl.BlockSpec((None,) + shape, lambda b, j, pt: (b,) + (0,) * len(shape))
    order = lambda j, r: j * ATTN_PAGES + r
    pages = lambda: _page_specs(ATTN_PAGES, PAGE * DSA_KV_HEADS, HEAD_DIM, layer, order)
    hshape = (DSA_HEADS, QPAD, LANES)
    grid_spec = pltpu.PrefetchScalarGridSpec(
        num_scalar_prefetch=1,
        grid=(nb, n_pages // ATTN_PAGES),
        in_specs=[per_b((QPAD, DSA_HEADS * HEAD_DIM)), per_b((n_pages + 1, QPAD, LANES))]
        + pages() + pages() + [per_b((PAGE, width)), per_b((PAGE, width))],
        out_specs=per_b((QPAD, DSA_HEADS * HEAD_DIM)),
        scratch_shapes=[pltpu.VMEM((DSA_KV_HEADS, DSA_GROUP * QPAD, HEAD_DIM), BF16),
                        pltpu.VMEM(hshape, F32), pltpu.VMEM(hshape, F32), pltpu.VMEM(hshape, F32)],
    )
    return pl.pallas_call(
        functools.partial(_dsa_sample_attn_kernel, n_pages=n_pages),
        grid_spec=grid_spec,
        out_shape=jax.ShapeDtypeStruct((nb, QPAD, DSA_HEADS * HEAD_DIM), F32),
        compiler_params=_cparams(("parallel", "arbitrary"), VMEM_LIMIT),
        name="dsa_sample_attention",
    )(page_table, q8, sel, *([cache_k] * ATTN_PAGES), *([cache_v] * ATTN_PAGES), k_new_pad, v_new_pad)


def _sb_sample_kernel(pt_ref, q_ref, *refs, n_new):
    k_refs = refs[:ATTN_PAGES]
    v_refs = refs[ATTN_PAGES:2 * ATTN_PAGES]
    kn_ref, vn_ref, o_ref, qh_ref, carry_ref, acc_ref = refs[2 * ATTN_PAGES:]
    j = pl.program_id(1)
    suffix_mat = _suffix_matrix()
    rows = SB_HEADS * QPAD

    def attend(k_of, v_of, strict):
        qh = qh_ref[...]
        z = jnp.concatenate([_dot_nt(qh, k_of(h).astype(BF16))[h * QPAD:(h + 1) * QPAD]
                             for h in range(SB_HEADS)], axis=0)
        carry, w = _sb_weights(z, carry_ref[...], suffix_mat, strict)
        wb = w.astype(BF16)
        pv = jnp.concatenate([_dot(wb, v_of(h).astype(BF16))[h * QPAD:(h + 1) * QPAD]
                              for h in range(SB_HEADS)], axis=0)
        carry_ref[...] = carry
        acc_ref[...] = acc_ref[...] + pv

    @pl.when(j == 0)
    def _():
        q = q_ref[...].astype(F32)
        qh_ref[...] = jnp.concatenate(
            [q[:, h * SB_HEAD_DIM:(h + 1) * SB_HEAD_DIM] for h in range(SB_HEADS)], axis=0).astype(BF16)
        carry_ref[...] = jnp.zeros(carry_ref.shape, F32)
        acc_ref[...] = jnp.zeros(acc_ref.shape, F32)
        row = lax.broadcasted_iota(I32, (rows, LANES), 0) % QPAD
        col = lax.broadcasted_iota(I32, (rows, LANES), 1)
        strict = (col < row) & (row < n_new)
        head = lambda ref: (lambda h: ref[:, h * SB_HEAD_DIM:(h + 1) * SB_HEAD_DIM])
        attend(head(kn_ref), head(vn_ref), strict)

    gather = lambda page_refs: (lambda h: jnp.concatenate(
        [_head_rows(r, h, SB_HEADS) for r in reversed(page_refs)], axis=0))
    attend(gather(k_refs), gather(v_refs), None)

    @pl.when(j == pl.num_programs(1) - 1)
    def _():
        for h in range(SB_HEADS):
            o_ref[:, h * SB_HEAD_DIM:(h + 1) * SB_HEAD_DIM] = acc_ref[h * QPAD:(h + 1) * QPAD, :]


def sb_sample_attention(page_table, q8, cache_k, cache_v, layer, k_new_pad, v_new_pad, n_new):
    nb, n_pages = page_table.shape
    assert n_pages % ATTN_PAGES == 0
    width = SB_HEADS * SB_HEAD_DIM
    per_b = lambda shape: pl.BlockSpec((None,) + shape, lambda b, j, pt: (b,) + (0,) * len(shape))
    order = lambda j, r: n_pages - 1 - (j * ATTN_PAGES + r)
    pages = lambda: _page_specs(ATTN_PAGES, PAGE * SB_HEADS, SB_HEAD_DIM, layer, order)
    grid_spec = pltpu.PrefetchScalarGridSpec(
        num_scalar_prefetch=1,
        grid=(nb, n_pages // ATTN_PAGES),
        in_specs=[per_b((QPAD, width))] + pages() + pages() + [per_b((PAGE, width)), per_b((PAGE, width))],
        out_specs=per_b((QPAD, width)),
        scratch_shapes=[pltpu.VMEM((SB_HEADS * QPAD, SB_HEAD_DIM), BF16),
                        pltpu.VMEM((SB_HEADS * QPAD, LANES), F32),
                        pltpu.VMEM((SB_HEADS * QPAD, SB_HEAD_DIM), F32)],
    )
    return pl.pallas_call(
        functools.partial(_sb_sample_kernel, n_new=n_new),
        grid_spec=grid_spec,
        out_shape=jax.ShapeDtypeStruct((nb, QPAD, width), F32),
        compiler_params=_cparams(("parallel", "arbitrary"), VMEM_LIMIT),
        name="sb_sample_attention",
    )(page_table, q8, *([cache_k] * ATTN_PAGES), *([cache_v] * ATTN_PAGES), k_new_pad, v_new_pad)


def _pad_rows(a, nb, n_new, rows):
    a = a.reshape(nb, n_new, a.shape[-1])
    return jnp.pad(a, ((0, 0), (0, rows - n_new), (0, 0)))


def _token_head_rows(cache):
    return cache.reshape(cache.shape[:2] + (cache.shape[2] * cache.shape[3], cache.shape[4]))


def dsa_sample(page_table, q, qi, wi, k_new, v_new, ki_new, cache_k, cache_v, cache_idx_k, layer, topk):
    nb = page_table.shape[0]
    n_new = q.shape[0] // nb
    sel = dsa_sample_select(page_table, _pad_rows(qi, nb, n_new, QPAD), _pad_rows(wi, nb, n_new, QPAD),
                            jnp.swapaxes(cache_idx_k, 2, 3), layer,
                            jnp.swapaxes(_pad_rows(ki_new, nb, n_new, PAGE), 1, 2), n_new, topk)
    o = dsa_sample_attention(page_table, _pad_rows(q, nb, n_new, QPAD), sel,
                             _token_head_rows(cache_k), _token_head_rows(cache_v), layer,
                             _pad_rows(k_new, nb, n_new, PAGE), _pad_rows(v_new, nb, n_new, PAGE))
    return o[:, :n_new].reshape(nb * n_new, -1)


def sb_sample(page_table, q, k_new, v_new, cache_k, cache_v, layer):
    nb = page_table.shape[0]
    n_new = q.shape[0] // nb
    o = sb_sample_attention(page_table, _pad_rows(q, nb, n_new, QPAD),
                            _token_head_rows(cache_k), _token_head_rows(cache_v), layer,
                            _pad_rows(k_new, nb, n_new, PAGE), _pad_rows(v_new, nb, n_new, PAGE), n_new)
    return o[:, :n_new].reshape(nb * n_new, -1)


S5_TILE_U = 128
S5_TILE_S = S5_TILE_U // S5_GROUP * S5_STATE
S5_NTILE = D_MODEL // S5_TILE_U
S5_TILES_PER_HALF = S5_NTILE // 2
S5_LANE_TILES = S5_HALF // LANES


def _s5_kernel(x_ref, g_ref, d_ref, bx_ref, ct_ref, are_ref, aim_ref, h0re_ref, h0im_ref,
               y_ref, hre_out, him_out, xre, xim, hre, him, *, nb, tc, t_last, split_input):
    c = pl.program_id(0)
    rows_half = (nb // 2) * tc

    @pl.when(c == 0)
    def _():
        hre[...] = h0re_ref[...]
        him[...] = h0im_ref[...]

    u = _rms(x_ref[...].reshape(rows_half, D_MODEL), g_ref[...])
    u_hi, u_lo = _split_bf16(u)
    lanes_per_tile = S5_TILE_S // LANES
    for tile in range(S5_NTILE):
        half, ctile = divmod(tile, S5_TILES_PER_HALF)
        rows = slice(half * rows_half, (half + 1) * rows_half)
        ucol = slice(tile * S5_TILE_U, (tile + 1) * S5_TILE_U)
        a_hi, a_lo = u_hi[:, ucol], u_lo[:, ucol]
        re = _dot(a_hi, bx_ref[0, tile])
        im = _dot(a_hi, bx_ref[2, tile])
        if split_input:
            re = re + _dot(a_lo, bx_ref[0, tile]) + _dot(a_hi, bx_ref[1, tile])
            im = im + _dot(a_lo, bx_ref[2, tile]) + _dot(a_hi, bx_ref[3, tile])
        for k in range(lanes_per_tile):
            lt = ctile * lanes_per_tile + k
            xre[lt, rows, :] = re[:, k * LANES:(k + 1) * LANES]
            xim[lt, rows, :] = im[:, k * LANES:(k + 1) * LANES]

    def step(t, carry):
        idx = pl.ds(t, nb, stride=tc)
        new = []
        for lt in range(S5_LANE_TILES):
            hr, hi = carry[lt]
            ar = are_ref[lt]
            ai = aim_ref[lt]
            nr = ar * hr - ai * hi + xre[lt, idx, :]
            ni = ar * hi + ai * hr + xim[lt, idx, :]
            xre[lt, idx, :] = nr
            xim[lt, idx, :] = ni
            new.append((nr, ni))
        return tuple(new)

    h_end = lax.fori_loop(0, tc, step, tuple((hre[lt], him[lt]) for lt in range(S5_LANE_TILES)))
    for lt in range(S5_LANE_TILES):
        hre[lt] = h_end[lt][0]
        him[lt] = h_end[lt][1]

    ys = []
    for tile in range(S5_NTILE):
        half, ctile = divmod(tile, S5_TILES_PER_HALF)
        rows = slice(half * rows_half, (half + 1) * rows_half)
        lts = range(ctile * lanes_per_tile, (ctile + 1) * lanes_per_tile)
        h_re = jnp.concatenate([xre[lt, rows, :] for lt in lts], axis=1).astype(BF16)
        h_im = jnp.concatenate([xim[lt, rows, :] for lt in lts], axis=1).astype(BF16)
        ys.append(_dot(h_re, ct_ref[0, tile]) + _dot(h_im, ct_ref[1, tile]))
    y = jnp.concatenate(ys, axis=1) + d_ref[...] * u
    y_ref[...] = y.reshape(y_ref.shape)

    @pl.when(c == t_last // tc)
    def _():
        idx = pl.ds(t_last % tc, nb, stride=tc)
        for lt in range(S5_LANE_TILES):
            hre_out[lt] = xre[lt, idx, :]
            him_out[lt] = xim[lt, idx, :]


def _s5_discretize(log_dt, a_re, a_im, b_re, b_im, c_re, c_im):
    dt = jnp.exp(log_dt)[:, None]
    decay = jnp.exp(dt * a_re)
    ang = dt * a_im
    ab_re = decay * jnp.cos(ang)
    ab_im = decay * jnp.sin(ang)
    den = a_re * a_re + a_im * a_im
    nr = ab_re - 1.0
    cf_re = (nr * a_re + ab_im * a_im) / den
    cf_im = (ab_im * a_re - nr * a_im) / den
    bx_re = cf_re[:, :, None] * b_re - cf_im[:, :, None] * b_im
    bx_im = cf_re[:, :, None] * b_im + cf_im[:, :, None] * b_re
    gpt = S5_TILE_U // S5_GROUP
    eye = jnp.eye(gpt, dtype=F32)

    def b_tiles(b):
        bt = b.reshape(S5_NTILE, gpt, S5_STATE, S5_GROUP)
        return jnp.einsum('tgpc,gh->tgchp', bt, eye).reshape(S5_NTILE, S5_TILE_U, S5_TILE_S)

    def c_tiles(cm):
        ctl = cm.reshape(S5_NTILE, gpt, S5_GROUP, S5_STATE)
        return jnp.einsum('tgcp,gh->tgphc', ctl, eye).reshape(S5_NTILE, S5_TILE_S, S5_TILE_U)

    bre_hi, bre_lo = _split_bf16(b_tiles(bx_re))
    bim_hi, bim_lo = _split_bf16(b_tiles(bx_im))
    bx = jnp.stack([bre_hi, bre_lo, bim_hi, bim_lo])
    ct = jnp.stack([c_tiles(c_re), -c_tiles(c_im)]).astype(BF16)
    return ab_re.reshape(-1), ab_im.reshape(-1), bx, ct


def _s5_rows(vec, n_seq):
    v = vec.reshape(2, 1, S5_LANE_TILES, LANES)
    return jnp.broadcast_to(v, (2, n_seq, S5_LANE_TILES, LANES)).transpose(2, 0, 1, 3).reshape(
        S5_LANE_TILES, 2 * n_seq, LANES)


def _s5_state_to_rows(h):
    n = h.shape[0]
    return h.reshape(n, 2, S5_LANE_TILES, LANES).transpose(2, 1, 0, 3).reshape(S5_LANE_TILES, 2 * n, LANES)


def _s5_rows_to_state(r):
    n = r.shape[1] // 2
    return r.reshape(S5_LANE_TILES, 2, n, LANES).transpose(2, 1, 0, 3).reshape(n, S5_GROUPS, S5_STATE)


def _s5_chunk_len(t_total):
    best = None
    for c in range(SUBLANES, QBLK + 1, 2 * SUBLANES):
        if t_total % c == 0:
            best = c
    assert best is not None
    return best


def s5_scan(x3, g, d_skip, disc, h0_re, h0_im, n_seq, tc, t_last, split_input):
    ab_re, ab_im, bx, ct = disc
    gdim, t_total, _ = x3.shape
    rchunk = n_seq * tc // gdim
    assert t_total % rchunk == 0
    nb = 2 * n_seq
    nchunks = t_total // rchunk
    st = (S5_LANE_TILES, nb, LANES)
    buf = (S5_LANE_TILES, nb * tc, LANES)
    y, hre, him = pl.pallas_call(
        functools.partial(_s5_kernel, nb=nb, tc=tc, t_last=t_last, split_input=split_input),
        grid=(nchunks,),
        in_specs=[pl.BlockSpec((gdim, rchunk, D_MODEL), lambda c: (0, c, 0)),
                  _const_spec((1, D_MODEL)), _const_spec((1, D_MODEL)),
                  _const_spec(bx.shape), _const_spec(ct.shape),
                  _const_spec(st), _const_spec(st), _const_spec(st), _const_spec(st)],
        out_specs=[pl.BlockSpec((gdim, rchunk, D_MODEL), lambda c: (0, c, 0)),
                   pl.BlockSpec(st, lambda c: (0, 0, 0)), pl.BlockSpec(st, lambda c: (0, 0, 0))],
        out_shape=[jax.ShapeDtypeStruct(x3.shape, F32),
                   jax.ShapeDtypeStruct(st, F32), jax.ShapeDtypeStruct(st, F32)],
        scratch_shapes=[pltpu.VMEM(buf, F32), pltpu.VMEM(buf, F32),
                        pltpu.VMEM(st, F32), pltpu.VMEM(st, F32)],
        compiler_params=_cparams(("arbitrary",), VMEM_LIMIT),
        name="s5_scan",
    )(x3, g.reshape(1, D_MODEL), d_skip.reshape(1, D_MODEL), bx, ct,
      _s5_rows(ab_re, n_seq), _s5_rows(ab_im, n_seq), _s5_state_to_rows(h0_re), _s5_state_to_rows(h0_im))
    return y, _s5_rows_to_state(hre), _s5_rows_to_state(him)


def kernel(x_prompt, x_sample, cache_dsa_k, cache_dsa_v, cache_dsa_idx_k, state_ssm_re, state_ssm_im,
           cache_sb_k, cache_sb_v, page_table, meta_tokens, g_mix, g_ffn,
           dsa_w_in, dsa_q_norm, dsa_k_norm, dsa_w_out,
           s5_log_dt, s5_a_re, s5_a_im, s5_b_re, s5_b_im, s5_c_re, s5_c_im, s5_d, s5_w_out,
           sb_w_in, sb_w_out, ffn_w_gu, ffn_w_down, moe_w_router, moe_b_router, moe_w_gu, moe_w_down):
    bsz, seq, _ = x_prompt.shape
    t_p = seq + N_META
    t_pad = -(-t_p // KEY_CHUNK) * KEY_CHUNK
    meta = jnp.broadcast_to(meta_tokens[None], (bsz, N_META, D_MODEL))
    xp = jnp.concatenate([meta, x_prompt, jnp.zeros((bsz, t_pad - t_p, D_MODEL), F32)], axis=1)
    xp = xp.reshape(bsz * t_pad, D_MODEL)
    nb, n_new, _ = x_sample.shape
    xs = x_sample.reshape(nb * n_new, D_MODEL)
    past_len = page_table.shape[1] * PAGE
    topk_p = min(TOPK_MAX, seq // 4)
    topk_s = min(TOPK_MAX, (past_len + n_new) // 4)
    tm_p = PROMPT_TILE
    tm_s = nb * n_new

    p3 = lambda a: a.reshape(bsz, t_pad, a.shape[-1])
    out_p = lambda a, *hd: a.reshape((bsz, t_pad) + hd)[:, :t_p]
    out_s = lambda a, *hd: a.reshape((nb, n_new) + hd)

    outs = {name: [] for name in ("dsa_k_p", "dsa_k_s", "dsa_v_p", "dsa_v_s", "dsa_i_p", "dsa_i_s",
                                  "ssm_re_p", "ssm_re_s", "ssm_im_p", "ssm_im_s",
                                  "sb_k_p", "sb_k_s", "sb_v_p", "sb_v_s")}
    ia = ib = ic = i_dense = i_moe = 0
    depth = g_mix.shape[0]
    for layer in range(depth):
        kind = layer % 3
        if kind == 0:
            w_in = jnp.concatenate(
                [dsa_w_in[ia], jnp.zeros((D_MODEL, DSA_IN_PAD - dsa_w_in.shape[2]), F32)], axis=1).astype(BF16)
            w_out = dsa_w_out[ia].astype(BF16)
            q, k, v, qi, ki, wi, kb, vb = dsa_project(xp, g_mix[layer], w_in, dsa_q_norm[ia], dsa_k_norm[ia], tm_p)
            o = dsa_attention_prompt(p3(q), p3(qi), p3(wi), p3(kb), p3(vb), p3(ki), topk_p)
            xp = out_project(xp, o.reshape(bsz * t_pad, D_MODEL), w_out, tm_p)
            qs, ks, vs, qis, kis, wis, _, _ = dsa_project(xs, g_mix[layer], w_in, dsa_q_norm[ia], dsa_k_norm[ia], tm_s)
            o_s = dsa_sample(page_table, qs, qis, wis, ks, vs, kis,
                             cache_dsa_k, cache_dsa_v, cache_dsa_idx_k, ia, topk_s)
            xs = out_project(xs, o_s.astype(BF16), w_out, tm_s)
            outs["dsa_k_p"].append(out_p(k, DSA_KV_HEADS, HEAD_DIM))
            outs["dsa_v_p"].append(out_p(v, DSA_KV_HEADS, HEAD_DIM))
            outs["dsa_i_p"].append(out_p(ki, IDX_DIM))
            outs["dsa_k_s"].append(out_s(ks, DSA_KV_HEADS, HEAD_DIM))
            outs["dsa_v_s"].append(out_s(vs, DSA_KV_HEADS, HEAD_DIM))
            outs["dsa_i_s"].append(out_s(kis, IDX_DIM))
            ia += 1
        elif kind == 1:
            disc = _s5_discretize(s5_log_dt[ib], s5_a_re[ib], s5_a_im[ib], s5_b_re[ib], s5_b_im[ib],
                                  s5_c_re[ib], s5_c_im[ib])
            w_out = s5_w_out[ib].astype(BF16)
            zeros = jnp.zeros((bsz, S5_GROUPS, S5_STATE), F32)
            yp, hre_p, him_p = s5_scan(p3(xp), g_mix[layer], s5_d[ib], disc, zeros, zeros, bsz,
                                       _s5_chunk_len(t_pad), t_p - 1, False)
            xp = s5_glu(xp, yp.reshape(bsz * t_pad, D_MODEL), w_out, tm_p)
            ys, hre_s, him_s = s5_scan(xs.reshape(1, nb * n_new, D_MODEL), g_mix[layer], s5_d[ib], disc,
                                       state_ssm_re[ib], state_ssm_im[ib], nb, n_new, n_new - 1, True)
            xs = s5_glu(xs, ys.reshape(nb * n_new, D_MODEL), w_out, tm_s)
            outs["ssm_re_p"].append(hre_p)
            outs["ssm_im_p"].append(him_p)
            outs["ssm_re_s"].append(hre_s)
            outs["ssm_im_s"].append(him_s)
            ib += 1
        else:
            w_in = sb_w_in[ic].astype(BF16)
            w_out = sb_w_out[ic].astype(BF16)
            q, k, v, kb, vb = sb_project(xp, g_mix[layer], w_in, tm_p)
            o = sb_attention_prompt(p3(q), p3(kb), p3(vb))
            xp = out_project(xp, o.reshape(bsz * t_pad, D_MODEL), w_out, tm_p)
            qs, ks, vs, _, _ = sb_project(xs, g_mix[layer], w_in, tm_s)
            o_s = sb_sample(page_table, qs, ks, vs, cache_sb_k, cache_sb_v, ic)
            xs = out_project(xs, o_s.astype(BF16), w_out, tm_s)
            outs["sb_k_p"].append(out_p(k, SB_HEADS, SB_HEAD_DIM))
            outs["sb_v_p"].append(out_p(v, SB_HEADS, SB_HEAD_DIM))
            outs["sb_k_s"].append(out_s(ks, SB_HEADS, SB_HEAD_DIM))
            outs["sb_v_s"].append(out_s(vs, SB_HEADS, SB_HEAD_DIM))
            ic += 1
        if layer % 2 == 0:
            wgu = ffn_w_gu[i_dense].astype(BF16)
            wd = ffn_w_down[i_dense].astype(BF16)
            xp = ffn(xp, g_ffn[layer], wgu, wd, tm_p)
            xs = ffn(xs, g_ffn[layer], wgu, wd, tm_s)
            i_dense += 1
        else:
            wr = jnp.concatenate([moe_w_router[i_moe], jnp.zeros((D_MODEL, LANES - N_EXPERTS), F32)], axis=1)
            br = jnp.concatenate([moe_b_router[i_moe], jnp.full((LANES - N_EXPERTS,), NEG_BIG, F32)])
            br = br.reshape(1, LANES)
            wgu = moe_w_gu[i_moe].astype(BF16)
            wd = moe_w_down[i_moe].astype(BF16)
            xp = moe_routed(xp, g_ffn[layer], wr, br, wgu, wd, MOE_TILE)
            xs = moe(xs, g_ffn[layer], wr, br, wgu, wd, tm_s)
            i_moe += 1

    y_prompt = xp.reshape(bsz, t_pad, D_MODEL)[:, N_META:t_p]
    y_sample = xs.reshape(nb, n_new, D_MODEL)
    stack = lambda name: jnp.stack(outs[name])
    return (y_prompt, y_sample,
            stack("dsa_k_p"), stack("dsa_k_s"), stack("dsa_v_p"), stack("dsa_v_s"),
            stack("dsa_i_p"), stack("dsa_i_s"),
            stack("ssm_re_p"), stack("ssm_re_s"), stack("ssm_im_p"), stack("ssm_im_s"),
            stack("sb_k_p"), stack("sb_k_s"), stack("sb_v_p"), stack("sb_v_s"))
```

```python
import functools
import math

import jax
import jax.numpy as jnp
from jax import lax
from jax.experimental import pallas as pl
from jax.experimental.pallas import tpu as pltpu

F32 = jnp.float32
BF16 = jnp.bfloat16
I32 = jnp.int32

D_MODEL = 1024
N_META = 16
EPS = 1e-6
DSA_HEADS = 8
DSA_KV_HEADS = 2
DSA_GROUP = DSA_HEADS // DSA_KV_HEADS
HEAD_DIM = 128
IDX_HEADS = 8
IDX_DIM = 64
TOPK_MAX = 256
S5_GROUP = 16
S5_GROUPS = 64
S5_STATE = 64
S5_NSTATE = S5_GROUPS * S5_STATE
S5_HALF = S5_NSTATE // 2
SB_HEADS = 8
SB_HEAD_DIM = 128
FF_DIM = 11 * D_MODEL // 4
N_EXPERTS = 8
PAGE = 128

LANES = 128
SUBLANES = 8
VMEM_LIMIT = 56 * 1024 * 1024

QBLK = 128
KEY_SUB = 3
KEY_CHUNK = KEY_SUB * QBLK
PROMPT_TILE = 512
DSA_IN_DIM = (DSA_HEADS + 2 * DSA_KV_HEADS) * HEAD_DIM + IDX_HEADS * IDX_DIM + IDX_DIM + IDX_HEADS
DSA_IN_PAD = -(-DSA_IN_DIM // LANES) * LANES
INT_MIN = -(2 ** 31)
INT_MAX = 2 ** 31 - 1
NEG_BIG = -1e30


def _cparams(sem, vmem=None):
    return pltpu.CompilerParams(dimension_semantics=sem, vmem_limit_bytes=vmem)


def _const_spec(shape):
    nd = len(shape)
    return pl.BlockSpec(shape, lambda *_: (0,) * nd, pipeline_mode=pl.Buffered(1))


def _rms(x, g):
    return x * lax.rsqrt(jnp.mean(x * x, axis=-1, keepdims=True) + EPS) * g


def _dot(a, b):
    return jnp.dot(a, b, preferred_element_type=F32)


def _dot_nt(a, b):
    return lax.dot_general(a, b, (((1,), (1,)), ((), ())), preferred_element_type=F32)


def _split_bf16(x):
    hi = x.astype(BF16)
    lo = (x - hi.astype(F32)).astype(BF16)
    return hi, lo


def _lane_tiles(x):
    return [x[:, u * LANES:(u + 1) * LANES] for u in range(x.shape[-1] // LANES)]


def _dsa_proj_kernel(x_ref, g_ref, w_ref, qg_ref, kg_ref,
                     q_ref, k_ref, v_ref, qi_ref, ki_ref, wi_ref, kb_ref, vb_ref, kip_ref):
    h = _rms(x_ref[...], g_ref[...]).astype(BF16)
    nq = DSA_HEADS * HEAD_DIM
    nkv = DSA_KV_HEADS * HEAD_DIM
    q = _dot(h, w_ref[:, 0:nq])
    scale = 1.0 / math.sqrt(HEAD_DIM)
    for hd in range(DSA_HEADS):
        sl = slice(hd * HEAD_DIM, (hd + 1) * HEAD_DIM)
        q_ref[:, sl] = (_rms(q[:, sl], qg_ref[...]) * scale).astype(BF16)
    k = _dot(h, w_ref[:, nq:nq + nkv])
    for hd in range(DSA_KV_HEADS):
        sl = slice(hd * HEAD_DIM, (hd + 1) * HEAD_DIM)
        kn = _rms(k[:, sl], kg_ref[...])
        k_ref[:, sl] = kn
        kb_ref[:, sl] = kn.astype(BF16)
    v = _dot(h, w_ref[:, nq + nkv:nq + 2 * nkv])
    v_ref[...] = v
    vb_ref[...] = v.astype(BF16)
    o = nq + 2 * nkv
    qi_ref[...] = _dot(h, w_ref[:, o:o + IDX_HEADS * IDX_DIM]).astype(BF16)
    o += IDX_HEADS * IDX_DIM
    tail = _dot(h, w_ref[:, o:o + LANES])
    ki_ref[...] = tail[:, 0:IDX_DIM]
    kip_ref[...] = tail[:, 0:IDX_DIM]
    wi_ref[...] = tail[:, IDX_DIM:IDX_DIM + IDX_HEADS]


def _seq_specs(b, t, tm, t_out):
    assert t % tm == 0 and 0 <= t - t_out < tm
    row = lambda n: pl.BlockSpec((None, tm, n), lambda bi, j: (bi, j, 0))
    full = lambda n, dt: jax.ShapeDtypeStruct((b, t, n), dt)
    kept = lambda n, dt: jax.ShapeDtypeStruct((b, t_out, n), dt)
    return row, full, kept


def dsa_project(x3, g, w_pad, qg, kg, tm, t_out):
    b, t, _ = x3.shape
    nq = DSA_HEADS * HEAD_DIM
    nkv = DSA_KV_HEADS * HEAD_DIM
    row, full, kept = _seq_specs(b, t, tm, t_out)
    return pl.pallas_call(
        _dsa_proj_kernel,
        grid=(b, t // tm),
        in_specs=[row(D_MODEL), _const_spec((1, D_MODEL)), _const_spec(w_pad.shape),
                  _const_spec((1, HEAD_DIM)), _const_spec((1, HEAD_DIM))],
        out_specs=[row(nq), row(nkv), row(nkv), row(IDX_HEADS * IDX_DIM), row(IDX_DIM), row(IDX_HEADS),
                   row(nkv), row(nkv), row(IDX_DIM)],
        out_shape=[full(nq, BF16), kept(nkv, F32), kept(nkv, F32), full(IDX_HEADS * IDX_DIM, BF16),
                   kept(IDX_DIM, F32), full(IDX_HEADS, F32), full(nkv, BF16), full(nkv, BF16),
                   full(IDX_DIM, F32)],
        compiler_params=_cparams(("parallel", "parallel"), VMEM_LIMIT),
        name="dsa_project",
    )(x3, g.reshape(1, D_MODEL), w_pad, qg.reshape(1, HEAD_DIM), kg.reshape(1, HEAD_DIM))


def _sb_proj_kernel(x_ref, g_ref, w_ref, q_ref, k_ref, v_ref, kb_ref, vb_ref):
    h = _rms(x_ref[...], g_ref[...]).astype(BF16)
    scale = 1.0 / math.sqrt(SB_HEAD_DIM)
    q_ref[...] = (_dot(h, w_ref[:, 0:D_MODEL]) * scale).astype(BF16)
    k = _dot(h, w_ref[:, D_MODEL:2 * D_MODEL])
    k_ref[...] = k
    kb_ref[...] = k.astype(BF16)
    v = _dot(h, w_ref[:, 2 * D_MODEL:3 * D_MODEL])
    v_ref[...] = v
    vb_ref[...] = v.astype(BF16)


def sb_project(x3, g, w, tm, t_out):
    b, t, _ = x3.shape
    row, full, kept = _seq_specs(b, t, tm, t_out)
    return pl.pallas_call(
        _sb_proj_kernel,
        grid=(b, t // tm),
        in_specs=[row(D_MODEL), _const_spec((1, D_MODEL)), _const_spec(w.shape)],
        out_specs=[row(D_MODEL)] * 5,
        out_shape=[full(D_MODEL, BF16), kept(D_MODEL, F32), kept(D_MODEL, F32),
                   full(D_MODEL, BF16), full(D_MODEL, BF16)],
        compiler_params=_cparams(("parallel", "parallel"), VMEM_LIMIT),
        name="sb_project",
    )(x3, g.reshape(1, D_MODEL), w)


def _out_proj_kernel(x_ref, o_ref, w_ref, y_ref):
    y_ref[...] = x_ref[...] + _dot(o_ref[...], w_ref[...])


def out_project(x, o, w, tm):
    m = x.shape[0]
    assert m % tm == 0
    row = pl.BlockSpec((tm, D_MODEL), lambda i: (i, 0))
    return pl.pallas_call(
        _out_proj_kernel,
        grid=(m // tm,),
        in_specs=[row, row, _const_spec(w.shape)],
        out_specs=row,
        out_shape=jax.ShapeDtypeStruct((m, D_MODEL), F32),
        compiler_params=_cparams(("parallel",), VMEM_LIMIT),
        name="out_project",
    )(x, o, w)


def _gelu_tanh(y):
    c = math.sqrt(2.0 / math.pi)
    return 0.5 * y * (1.0 + jnp.tanh(c * (y + 0.044715 * (y * y * y))))


def _s5_glu_kernel(x_ref, y_ref, w_ref, o_ref):
    z = _gelu_tanh(y_ref[...]).astype(BF16)
    a = _dot(z, w_ref[:, 0:D_MODEL])
    g = _dot(z, w_ref[:, D_MODEL:2 * D_MODEL])
    o_ref[...] = x_ref[...] + a * jax.nn.sigmoid(g)


def s5_glu(x, y, w, tm):
    m = x.shape[0]
    assert m % tm == 0
    row = pl.BlockSpec((tm, D_MODEL), lambda i: (i, 0))
    return pl.pallas_call(
        _s5_glu_kernel,
        grid=(m // tm,),
        in_specs=[row, row, _const_spec(w.shape)],
        out_specs=row,
        out_shape=jax.ShapeDtypeStruct((m, D_MODEL), F32),
        compiler_params=_cparams(("parallel",), VMEM_LIMIT),
        name="s5_glu",
    )(x, y, w)


FF_CHUNK = 256


FF_NCHUNK = FF_DIM // FF_CHUNK


def _swiglu_acc(h, wgu_ref, wd_ref, acc, between=None):
    for c in range(FF_NCHUNK):
        lo = c * FF_CHUNK
        g = _dot(h, wgu_ref[:, lo:lo + FF_CHUNK])
        u = _dot(h, wgu_ref[:, FF_DIM + lo:FF_DIM + lo + FF_CHUNK])
        a = (g * jax.nn.sigmoid(g) * u).astype(BF16)
        acc = acc + _dot(a, wd_ref[lo:lo + FF_CHUNK, :])
        if between is not None:
            between(c)
    return acc


def _ffn_kernel(x_ref, g_ref, wgu_ref, wd_ref, o_ref):
    x = x_ref[...]
    h = _rms(x, g_ref[...]).astype(BF16)
    o_ref[...] = _swiglu_acc(h, wgu_ref, wd_ref, x)


def ffn(x, g, wgu, wd, tm):
    m = x.shape[0]
    assert m % tm == 0
    row = pl.BlockSpec((tm, D_MODEL), lambda i: (i, 0))
    return pl.pallas_call(
        _ffn_kernel,
        grid=(m // tm,),
        in_specs=[row, _const_spec((1, D_MODEL)), _const_spec(wgu.shape), _const_spec(wd.shape)],
        out_specs=row,
        out_shape=jax.ShapeDtypeStruct((m, D_MODEL), F32),
        compiler_params=_cparams(("parallel",), VMEM_LIMIT),
        name="ffn",
    )(x, g.reshape(1, D_MODEL), wgu, wd)


def _moe_kernel(x_ref, g_ref, wr_ref, br_ref, wgu_ref, wd_ref, o_ref, h_scr, comb_scr):
    e = pl.program_id(1)

    @pl.when(e == 0)
    def _():
        x = x_ref[...]
        hf = _rms(x, g_ref[...])
        h_scr[...] = hf.astype(BF16)
        h_hi, h_lo = _split_bf16(hf)
        w_hi, w_lo = _split_bf16(wr_ref[...])
        logits = _dot(h_hi, w_hi) + _dot(h_lo, w_hi) + _dot(h_hi, w_lo) + br_ref[...]
        lane = lax.broadcasted_iota(I32, logits.shape, 1)
        m1 = jnp.max(logits, axis=-1, keepdims=True)
        i1 = jnp.min(jnp.where(logits == m1, lane, LANES), axis=-1, keepdims=True)
        rest = jnp.where(lane == i1, NEG_BIG, logits)
        m2 = jnp.max(rest, axis=-1, keepdims=True)
        i2 = jnp.min(jnp.where(rest == m2, lane, LANES), axis=-1, keepdims=True)
        e2 = jnp.exp(m2 - m1)
        g1 = 1.0 / (1.0 + e2)
        g2 = e2 / (1.0 + e2)
        comb_scr[...] = jnp.where(lane == i1, g1, 0.0) + jnp.where(lane == i2, g2, 0.0)
        o_ref[...] = x

    lane = lax.broadcasted_iota(I32, comb_scr.shape, 1)
    gate = jnp.sum(jnp.where(lane == e, comb_scr[...], 0.0), axis=-1, keepdims=True)
    y = _swiglu_acc(h_scr[...], wgu_ref.at[0], wd_ref.at[0], jnp.zeros(o_ref.shape, F32))
    o_ref[...] = o_ref[...] + gate * y


def moe(x, g, w_router_pad, b_router_pad, wgu, wd, tm):
    m = x.shape[0]
    assert m % tm == 0
    row = pl.BlockSpec((tm, D_MODEL), lambda i, e: (i, 0))
    cst = lambda shape: pl.BlockSpec(shape, lambda i, e: (0,) * len(shape), pipeline_mode=pl.Buffered(1))
    return pl.pallas_call(
        _moe_kernel,
        grid=(m // tm, N_EXPERTS),
        in_specs=[row, cst((1, D_MODEL)), cst((D_MODEL, LANES)), cst((1, LANES)),
                  pl.BlockSpec((1, D_MODEL, 2 * FF_DIM), lambda i, e: (e, 0, 0)),
                  pl.BlockSpec((1, FF_DIM, D_MODEL), lambda i, e: (e, 0, 0))],
        out_specs=row,
        out_shape=jax.ShapeDtypeStruct((m, D_MODEL), F32),
        scratch_shapes=[pltpu.VMEM((tm, D_MODEL), BF16), pltpu.VMEM((tm, LANES), F32)],
        compiler_params=_cparams(("parallel", "arbitrary"), VMEM_LIMIT),
        name="moe",
    )(x, g.reshape(1, D_MODEL), w_router_pad, b_router_pad, wgu, wd)


MOE_TILE = 512
ROUTE_E1, ROUTE_E2, ROUTE_G1, ROUTE_G2 = 0, 1, 2, 3


def _route_kernel(x_ref, g_ref, wr_ref, br_ref, o_ref):
    hf = _rms(x_ref[...], g_ref[...])
    h_hi, h_lo = _split_bf16(hf)
    w_hi, w_lo = _split_bf16(wr_ref[...])
    logits = _dot(h_hi, w_hi) + _dot(h_lo, w_hi) + _dot(h_hi, w_lo) + br_ref[...]
    lane = lax.broadcasted_iota(I32, logits.shape, 1)
    m1 = jnp.max(logits, axis=-1, keepdims=True)
    i1 = jnp.min(jnp.where(logits == m1, lane, LANES), axis=-1, keepdims=True)
    rest = jnp.where(lane == i1, NEG_BIG, logits)
    m2 = jnp.max(rest, axis=-1, keepdims=True)
    i2 = jnp.min(jnp.where(rest == m2, lane, LANES), axis=-1, keepdims=True)
    e2 = jnp.exp(m2 - m1)
    g1 = 1.0 / (1.0 + e2)
    g2 = e2 / (1.0 + e2)
    o_ref[...] = (jnp.where(lane == ROUTE_E1, i1.astype(F32), 0.0) + jnp.where(lane == ROUTE_E2, i2.astype(F32), 0.0)
                  + jnp.where(lane == ROUTE_G1, g1, 0.0) + jnp.where(lane == ROUTE_G2, g2, 0.0))


def moe_route(x, g, w_router_pad, b_router_pad, tm):
    m = x.shape[0]
    assert m % tm == 0
    return pl.pallas_call(
        _route_kernel,
        grid=(m // tm,),
        in_specs=[pl.BlockSpec((tm, D_MODEL), lambda i: (i, 0)), _const_spec((1, D_MODEL)),
                  _const_spec((D_MODEL, LANES)), _const_spec((1, LANES))],
        out_specs=pl.BlockSpec((tm, LANES), lambda i: (i, 0)),
        out_shape=jax.ShapeDtypeStruct((m, LANES), F32),
        compiler_params=_cparams(("parallel",), VMEM_LIMIT),
        name="moe_route",
    )(x, g.reshape(1, D_MODEL), w_router_pad, b_router_pad)


def _row_gather_start(idx_ref, n_rows, src_hbm, dst, sem):
    def body(r, _):
        pltpu.make_async_copy(src_hbm.at[pl.ds(idx_ref[0, r], 1)], dst.at[pl.ds(r, 1)], sem).start()
        return 0
    lax.fori_loop(0, n_rows, body, 0, unroll=8)


def _row_gather_wait(n_rows, src_hbm, dst, sem):
    pltpu.make_async_copy(src_hbm.at[pl.ds(0, n_rows)], dst, sem).wait()


def _moe_expert_kernel(te_ref, nu_ref, src_cur, src_next, x_hbm, g_ref, wgu_ref, wd_ref, o_ref, xbuf, sem):
    i = pl.program_id(0)
    slot = i % 2
    n_used = nu_ref[0]
    tm = o_ref.shape[0]

    per_chunk = -(-tm // FF_NCHUNK)

    @pl.when(i == 0)
    def _():
        _row_gather_start(src_cur, tm, x_hbm, xbuf.at[0], sem.at[0])

    @pl.when(i < n_used)
    def _():
        _row_gather_wait(tm, x_hbm, xbuf.at[slot], sem.at[slot])
        h = _rms(xbuf[slot], g_ref[...]).astype(BF16)

        def gather_next(c):
            for r in range(c * per_chunk, min((c + 1) * per_chunk, tm)):
                pltpu.make_async_copy(x_hbm.at[pl.ds(src_next[0, r], 1)], xbuf.at[1 - slot, pl.ds(r, 1)],
                                      sem.at[1 - slot]).start()

        o_ref[...] = _swiglu_acc(h, wgu_ref.at[0], wd_ref.at[0], jnp.zeros(o_ref.shape, F32), gather_next)

    @pl.when(i == n_used)
    def _():
        _row_gather_wait(tm, x_hbm, xbuf.at[slot], sem.at[slot])

    @pl.when(i >= n_used)
    def _():
        o_ref[...] = jnp.zeros(o_ref.shape, F32)


def moe_experts(x, g, src_rows, tile_expert, n_used, wgu, wd, tm):
    n_tiles = src_rows.shape[0]
    cst = lambda shape: pl.BlockSpec(shape, lambda i, te, nu: (0,) * len(shape), pipeline_mode=pl.Buffered(1))
    smem_rows = lambda off: pl.BlockSpec((None, 1, tm), lambda i, te, nu: (jnp.minimum(i + off, n_tiles - 1), 0, 0),
                                         memory_space=pltpu.SMEM)
    grid_spec = pltpu.PrefetchScalarGridSpec(
        num_scalar_prefetch=2,
        grid=(n_tiles,),
        in_specs=[smem_rows(0), smem_rows(1), pl.BlockSpec(memory_space=pl.ANY), cst((1, D_MODEL)),
                  pl.BlockSpec((1, D_MODEL, 2 * FF_DIM), lambda i, te, nu: (te[i], 0, 0)),
                  pl.BlockSpec((1, FF_DIM, D_MODEL), lambda i, te, nu: (te[i], 0, 0))],
        out_specs=pl.BlockSpec((tm, D_MODEL), lambda i, te, nu: (i, 0)),
        scratch_shapes=[pltpu.VMEM((2, tm, D_MODEL), F32), pltpu.SemaphoreType.DMA((2,))],
    )
    return pl.pallas_call(
        _moe_expert_kernel,
        grid_spec=grid_spec,
        out_shape=jax.ShapeDtypeStruct((n_tiles * tm, D_MODEL), F32),
        compiler_params=_cparams(("arbitrary",), VMEM_LIMIT),
        name="moe_experts",
    )(tile_expert, n_used, src_rows, src_rows, x, g.reshape(1, D_MODEL), wgu, wd)


def _moe_combine_kernel(dst_cur, dst_next, x_ref, route_ref, y_hbm, o_ref, ybuf, sem):
    i = pl.program_id(0)
    slot = i % 2
    tm = o_ref.shape[0]
    rows = 2 * tm

    @pl.when(i == 0)
    def _():
        _row_gather_start(dst_cur, rows, y_hbm, ybuf.at[0], sem.at[0])

    @pl.when(i + 1 < pl.num_programs(0))
    def _():
        _row_gather_start(dst_next, rows, y_hbm, ybuf.at[1 - slot], sem.at[1 - slot])

    _row_gather_wait(rows, y_hbm, ybuf.at[slot], sem.at[slot])
    route = route_ref[...]
    g1 = route[:, ROUTE_G1:ROUTE_G1 + 1]
    g2 = route[:, ROUTE_G2:ROUTE_G2 + 1]
    o_ref[...] = x_ref[...] + g1 * ybuf[slot, 0:tm, :] + g2 * ybuf[slot, tm:rows, :]


def moe_combine(x, route, y_sorted, dest_rows, tm):
    m = x.shape[0]
    assert m % tm == 0
    n = m // tm
    smem_rows = lambda off: pl.BlockSpec((None, 1, 2 * tm), lambda i: (jnp.minimum(i + off, n - 1), 0, 0),
                                         memory_space=pltpu.SMEM)
    return pl.pallas_call(
        _moe_combine_kernel,
        grid=(n,),
        in_specs=[smem_rows(0), smem_rows(1), pl.BlockSpec((tm, D_MODEL), lambda i: (i, 0)),
                  pl.BlockSpec((tm, LANES), lambda i: (i, 0)), pl.BlockSpec(memory_space=pl.ANY)],
        out_specs=pl.BlockSpec((tm, D_MODEL), lambda i: (i, 0)),
        out_shape=jax.ShapeDtypeStruct((m, D_MODEL), F32),
        scratch_shapes=[pltpu.VMEM((2, 2 * tm, D_MODEL), F32), pltpu.SemaphoreType.DMA((2,))],
        compiler_params=_cparams(("arbitrary",), VMEM_LIMIT),
        name="moe_combine",
    )(dest_rows, dest_rows, x, route, y_sorted)


def moe_routed(x, g, w_router_pad, b_router_pad, wgu, wd, tm):
    m = x.shape[0]
    route = moe_route(x, g, w_router_pad, b_router_pad, tm)
    experts = route[:, :2].astype(I32)
    flat = experts.reshape(-1)
    onehot = (flat[:, None] == jnp.arange(N_EXPERTS, dtype=I32)[None, :]).astype(I32)
    csum = jnp.cumsum(onehot, axis=0)
    rank = jnp.take_along_axis(csum, flat[:, None], axis=1)[:, 0] - 1
    tiles_per_expert = (csum[-1] + tm - 1) // tm
    tile_end = jnp.cumsum(tiles_per_expert)
    row_start = (tile_end - tiles_per_expert) * tm
    dest = row_start[flat] + rank
    n_tiles = 2 * m // tm + N_EXPERTS
    src = jnp.zeros((n_tiles * tm,), I32).at[dest].set(jnp.arange(2 * m, dtype=I32) // 2, unique_indices=True)
    tile_ids = jnp.arange(n_tiles, dtype=I32)
    tile_expert = jnp.minimum(jnp.sum((tile_end[None, :] <= tile_ids[:, None]).astype(I32), axis=1), N_EXPERTS - 1)
    n_used = tile_end[-1:].astype(I32)
    y_sorted = moe_experts(x, g, src.reshape(n_tiles, 1, tm), tile_expert, n_used, wgu, wd, tm)
    dest_rows = dest.reshape(m // tm, tm, 2).transpose(0, 2, 1).reshape(m // tm, 1, 2 * tm)
    return moe_combine(x, route, y_sorted, dest_rows, tm)


NEG_INF_CODE = INT_MIN + 0x7FFFFF


def _code_to_float(code):
    return pltpu.bitcast(code ^ ((code >> 31) & 0x7FFFFFFF), F32)


def _kth_largest(count_ge, topk, shape):
    zero = jnp.zeros(shape, I32)
    c0 = jnp.where(count_ge(_code_to_float(zero)) >= topk, zero, jnp.full(shape, INT_MIN, I32))

    def body(k, code):
        cand = code | jnp.left_shift(jnp.int32(1), 30 - k)
        return jnp.where(count_ge(_code_to_float(cand)) >= topk, cand, code)

    code = lax.fori_loop(0, 31, body, c0)
    return _code_to_float(jnp.maximum(code, NEG_INF_CODE))


def _indexer_scores(lg, wib_ref, rows):
    tiles = []
    for u in range(lg.shape[1] // LANES):
        cols = slice(u * LANES, (u + 1) * LANES)
        su = wib_ref[0] * jnp.maximum(lg[0:rows, cols], 0.0)
        for h in range(1, IDX_HEADS):
            su = su + wib_ref[h] * jnp.maximum(lg[h * rows:(h + 1) * rows, cols], 0.0)
        tiles.append(su)
    return tiles


def _count_matrix():
    r = lax.broadcasted_iota(I32, (QBLK, 2 * QBLK), 0)
    c = lax.broadcasted_iota(I32, (QBLK, 2 * QBLK), 1)
    return jnp.where((r < c) | (c >= QBLK), 1.0, 0.0).astype(BF16)


def _softmax_update(groups, sels, m_ref, l_ref, acc_ref, rows):
    old = {hd: (m_ref[hd], l_ref[hd], acc_ref[hd]) for _, _, heads in groups for hd in heads}
    new = {}
    for s, v, heads in groups:
        ps = []
        m_news = []
        for j, hd in enumerate(heads):
            tiles = _lane_tiles(s[j * rows:(j + 1) * rows])
            mx = jnp.where(sels[0], tiles[0], NEG_BIG)
            for sel, tile in zip(sels[1:], tiles[1:]):
                mx = jnp.maximum(mx, jnp.where(sel, tile, NEG_BIG))
            m_new = jnp.maximum(old[hd][0], jnp.max(mx, axis=-1, keepdims=True))
            p = [jnp.where(sel, jnp.exp(tile - m_new), 0.0) for sel, tile in zip(sels, tiles)]
            ps.append(jnp.concatenate(p, axis=1).astype(BF16) if len(p) > 1 else p[0].astype(BF16))
            m_news.append(m_new)
        pv = _dot(jnp.concatenate(ps, axis=0), jnp.concatenate([v, jnp.ones(v.shape, BF16)], axis=1))
        for j, hd in enumerate(heads):
            m_old, l_old, acc_old = old[hd]
            blk = pv[j * rows:(j + 1) * rows]
            alpha = jnp.exp(m_old - m_news[j])
            new[hd] = (m_news[j], alpha * l_old + blk[:, LANES:], alpha * acc_old + blk[:, :LANES])
    for hd, (m_new, l_new, acc_new) in new.items():
        m_ref[hd] = m_new
        l_ref[hd] = l_new
        acc_ref[hd] = acc_new


REDUCE_ROWS = 4 * SUBLANES


def _group_reduce(x, op):
    rows, nq = x.shape
    part = op(x.reshape(rows // REDUCE_ROWS, REDUCE_ROWS, nq), axis=0)
    return op(part, axis=0, keepdims=True)


def _prefix_rows(x, carry):
    rows, nq = x.shape
    groups = rows // SUBLANES
    x3 = x.reshape(groups, SUBLANES, nq)
    r = lax.broadcasted_iota(I32, x3.shape, 1)
    s = x3
    for k in (1, 2, 4):
        s = s + jnp.where(r >= k, pltpu.roll(s, k, axis=1), 0.0)
    earlier = [None] * groups
    for v in range(groups):
        earlier[v] = carry
        carry = carry + s[v, SUBLANES - 1:SUBLANES, :]
    return (s - x3 + jnp.stack(earlier, axis=0)).reshape(rows, nq), carry


def _dsa_prompt_kernel(q_ref, qi_ref, wi_ref, k_ref, vt_ref, ki_ref, o_ref,
                       score_ref, m_ref, l_ref, acc_ref, *, topk):
    i = pl.program_id(1)
    nsup = i // KEY_SUB + 1
    shape = (KEY_CHUNK, QBLK)
    key_iota = lax.broadcasted_iota(I32, shape, 0)
    q_pos = i * QBLK + lax.broadcasted_iota(I32, shape, 1)
    row = (1, QBLK)

    qi = qi_ref[...].astype(F32)
    qih_t = jnp.concatenate([qi[:, h * IDX_DIM:(h + 1) * IDX_DIM].T for h in range(IDX_HEADS)],
                            axis=1).astype(BF16)
    wi_t = wi_ref[...].T

    def score_super(sc, _):
        start = pl.multiple_of(sc * KEY_CHUNK, KEY_CHUNK)
        lg = _dot(ki_ref[pl.ds(start, KEY_CHUNK), :].astype(BF16), qih_t)
        s = wi_t[0:1, :] * jnp.maximum(lg[:, 0:QBLK], 0.0)
        for h in range(1, IDX_HEADS):
            s = s + wi_t[h:h + 1, :] * jnp.maximum(lg[:, h * QBLK:(h + 1) * QBLK], 0.0)
        score_ref[sc] = jnp.where(start + key_iota <= q_pos, s, -jnp.inf)
        return 0

    lax.fori_loop(0, nsup, score_super, 0)

    def count(hit_fn):
        def body(sc, acc):
            hit = jnp.where(hit_fn(score_ref[sc]), 1.0, 0.0)
            return acc + jnp.sum(hit.reshape(KEY_CHUNK // REDUCE_ROWS, REDUCE_ROWS, QBLK), axis=0)
        acc = lax.fori_loop(0, nsup, body, jnp.zeros((REDUCE_ROWS, QBLK), F32))
        return jnp.sum(acc, axis=0, keepdims=True)

    def tie_cutoff(t, need):
        def body(sc, state):
            carry, cut = state
            eq = score_ref[sc] == t
            before, carry = _prefix_rows(jnp.where(eq, 1.0, 0.0), carry)
            pos = jnp.where(eq & (before < need), sc * KEY_CHUNK + key_iota, -1)
            return carry, jnp.maximum(cut, _group_reduce(pos, jnp.max))
        _, cut = lax.fori_loop(0, nsup, body, (jnp.zeros(row, F32), jnp.full(row, -1, I32)))
        return cut

    def threshold():
        t = _kth_largest(lambda x: count(lambda s: s >= x), float(topk), row)
        need = float(topk) - count(lambda s: s > t)
        ties = count(lambda s: s == t)
        partial = jnp.max(jnp.where(ties > need, 1.0, 0.0)) > 0.0
        return t, lax.cond(partial, lambda: tie_cutoff(t, need), lambda: jnp.full(row, INT_MAX, I32))

    t, cut = lax.cond((i + 1) * QBLK <= topk,
                      lambda: (jnp.full(row, -jnp.inf, F32), jnp.full(row, INT_MAX, I32)), threshold)

    q = q_ref[...].astype(F32)
    hd = lambda h: slice(h * HEAD_DIM, (h + 1) * HEAD_DIM)
    qts = [jnp.concatenate([q[:, hd(n * DSA_GROUP + g)].T for g in range(DSA_GROUP)], axis=1).astype(BF16)
           for n in range(DSA_KV_HEADS)]
    m_ref[...] = jnp.full(m_ref.shape, NEG_BIG, F32)
    l_ref[...] = jnp.zeros(l_ref.shape, F32)
    acc_ref[...] = jnp.zeros(acc_ref.shape, F32)
    ones_rows = jnp.ones((SUBLANES, KEY_CHUNK), BF16)

    def attend(sc, diagonal):
        start = pl.multiple_of(sc * KEY_CHUNK, KEY_CHUNK)
        s_idx = score_ref[sc]
        key_pos = start + key_iota
        sel = (s_idx > t) | ((s_idx == t) & (key_pos <= cut))
        if diagonal:
            sel = sel & (key_pos <= q_pos)
        kc = k_ref[pl.ds(start, KEY_CHUNK), :]
        vt = vt_ref[sc]
        old = [(m_ref[n], l_ref[n], acc_ref[n]) for n in range(DSA_KV_HEADS)]
        sts = [[_dot(kc[:, hd(n)], qts[n][:, g * QBLK:(g + 1) * QBLK]) for g in range(DSA_GROUP)]
               for n in range(DSA_KV_HEADS)]
        new = []
        for n in range(DSA_KV_HEADS):
            m_old, l_old, acc_old = old[n]
            ps, m_news = [], []
            for g in range(DSA_GROUP):
                tile = jnp.where(sel, sts[n][g], NEG_BIG)
                m_new = jnp.maximum(m_old[:, g * QBLK:(g + 1) * QBLK], _group_reduce(tile, jnp.max))
                ps.append(jnp.exp(tile - m_new).astype(BF16))
                m_news.append(m_new)
            m_new = jnp.concatenate(m_news, axis=1)
            alpha = jnp.exp(m_old - m_new)
            pv = _dot(jnp.concatenate([vt[hd(n), :], ones_rows], axis=0), jnp.concatenate(ps, axis=1))
            new.append((m_new, alpha * l_old + pv[HEAD_DIM:HEAD_DIM + 1], alpha * acc_old + pv[:HEAD_DIM]))
        for n in range(DSA_KV_HEADS):
            m_ref[n], l_ref[n], acc_ref[n] = new[n]

    def body(sc, _):
        attend(sc, False)
        return 0

    lax.fori_loop(0, nsup - 1, body, 0)
    attend(nsup - 1, True)
    for n in range(DSA_KV_HEADS):
        o_t = acc_ref[n] / l_ref[n]
        for g in range(DSA_GROUP):
            o_ref[:, hd(n * DSA_GROUP + g)] = o_t[:, g * QBLK:(g + 1) * QBLK].T.astype(BF16)


def dsa_attention_prompt(q, qi, wi, k, v, ki, topk):
    b, t, _ = q.shape
    assert t % KEY_CHUNK == 0
    nsc = t // KEY_CHUNK
    vt = jnp.swapaxes(v.reshape(b, nsc, KEY_CHUNK, v.shape[2]), 2, 3)
    blk = lambda n: pl.BlockSpec((None, QBLK, n), lambda bi, i: (bi, i, 0))
    full = lambda n: pl.BlockSpec((None, t, n), lambda bi, i: (bi, 0, 0))
    width = DSA_GROUP * QBLK
    return pl.pallas_call(
        functools.partial(_dsa_prompt_kernel, topk=topk),
        grid=(b, t // QBLK),
        in_specs=[blk(q.shape[2]), blk(qi.shape[2]), blk(wi.shape[2]), full(k.shape[2]),
                  pl.BlockSpec((None, nsc, v.shape[2], KEY_CHUNK), lambda bi, i: (bi, 0, 0, 0)),
                  full(ki.shape[2])],
        out_specs=blk(q.shape[2]),
        out_shape=jax.ShapeDtypeStruct(q.shape, BF16),
        scratch_shapes=[pltpu.VMEM((nsc, KEY_CHUNK, QBLK), F32),
                        pltpu.VMEM((DSA_KV_HEADS, 1, width), F32),
                        pltpu.VMEM((DSA_KV_HEADS, 1, width), F32),
                        pltpu.VMEM((DSA_KV_HEADS, HEAD_DIM, width), F32)],
        compiler_params=_cparams(("parallel", "arbitrary"), VMEM_LIMIT),
        name="dsa_attention_prompt",
    )(q, qi, wi, k, vt, ki)


def _suffix_matrix():
    r = lax.broadcasted_iota(I32, (QBLK, 2 * QBLK), 0)
    c = lax.broadcasted_iota(I32, (QBLK, 2 * QBLK), 1)
    return jnp.where((r > c) | (c >= QBLK), 1.0, 0.0).astype(BF16)


def _sb_weights(z, carry, suffix_mat, strict):
    rows = z.shape[0]
    lk = -(jnp.maximum(z, 0.0) + jnp.log(1.0 + jnp.exp(-jnp.abs(z))))
    lkm = lk if strict is None else jnp.where(strict, lk, 0.0)
    hi, lo = _split_bf16(lkm)
    his, los = _lane_tiles(hi), _lane_tiles(lo)
    nsub = len(his)
    sufs = [None] * nsub
    for u in reversed(range(nsub)):
        r2 = _dot(jnp.concatenate([his[u], los[u]], axis=0), suffix_mat)
        ru = r2[:rows] + r2[rows:]
        sufs[u] = ru[:, :QBLK] + carry
        carry = carry + ru[:, QBLK:]
    suffix = jnp.concatenate(sufs, axis=1) if nsub > 1 else sufs[0]
    w = jnp.exp(z + lk + suffix)
    if strict is not None:
        w = jnp.where(strict, w, 0.0)
    return carry, w


SB_HEADS_PER_STEP = 4


def _sb_weights_t(zt, carry, strict):
    rows, nq = zt.shape
    groups = rows // SUBLANES
    lk = -(jnp.maximum(zt, 0.0) + jnp.log(1.0 + jnp.exp(-jnp.abs(zt))))
    lkm = lk if strict is None else jnp.where(strict, lk, 0.0)
    x3 = lkm.reshape(groups, SUBLANES, nq)
    r = lax.broadcasted_iota(I32, x3.shape, 1)
    s = x3
    for k in (1, 2, 4):
        s = s + jnp.where(r < SUBLANES - k, pltpu.roll(s, SUBLANES - k, axis=1), 0.0)
    later = [None] * groups
    for v in reversed(range(groups)):
        later[v] = carry
        carry = carry + s[v, 0:1, :]
    suffix = (s - x3 + jnp.stack(later, axis=0)).reshape(rows, nq)
    w = jnp.exp(zt + lk + suffix)
    if strict is not None:
        w = jnp.where(strict, w, 0.0)
    return carry, w


def _sb_prompt_kernel(q_ref, k_ref, vt_ref, o_ref, carry_ref, acc_ref):
    i = pl.program_id(2)
    top = i // KEY_SUB
    q = q_ref[...]
    heads = range(SB_HEADS_PER_STEP)
    hsl = lambda h: slice(h * SB_HEAD_DIM, (h + 1) * SB_HEAD_DIM)
    qts = [q[:, hsl(h)].astype(F32).T.astype(BF16) for h in heads]
    shape = (KEY_CHUNK, QBLK)
    key_iota = lax.broadcasted_iota(I32, shape, 0)
    q_pos = i * QBLK + lax.broadcasted_iota(I32, shape, 1)
    carry_ref[...] = jnp.zeros(carry_ref.shape, F32)
    acc_ref[...] = jnp.zeros(acc_ref.shape, F32)

    def run(sc, masked):
        start = pl.multiple_of(sc * KEY_CHUNK, KEY_CHUNK)
        kc = k_ref[pl.ds(start, KEY_CHUNK), :]
        vt = vt_ref[sc]
        strict = (start + key_iota < q_pos) if masked else None
        old = [(carry_ref[h], acc_ref[h]) for h in heads]
        new = []
        zts = [_dot(kc[:, hsl(h)], qts[h]) for h in heads]
        for h in heads:
            carry, w = _sb_weights_t(zts[h], old[h][0], strict)
            new.append((carry, old[h][1] + _dot(vt[hsl(h), :], w.astype(BF16))))
        for h in heads:
            carry_ref[h] = new[h][0]
            acc_ref[h] = new[h][1]

    run(top, True)

    def body(j, _):
        run(top - 1 - j, False)
        return 0

    lax.fori_loop(0, top, body, 0)
    for h in heads:
        o_ref[:, hsl(h)] = acc_ref[h].T.astype(BF16)


def sb_attention_prompt(q, k, v):
    b, t, _ = q.shape
    assert t % KEY_CHUNK == 0
    nsc = t // KEY_CHUNK
    width = SB_HEADS_PER_STEP * SB_HEAD_DIM
    vt = jnp.swapaxes(v.reshape(b, nsc, KEY_CHUNK, v.shape[2]), 2, 3)
    blk = pl.BlockSpec((None, QBLK, width), lambda bi, h, i: (bi, i, h))
    full = pl.BlockSpec((None, t, width), lambda bi, h, i: (bi, 0, h))
    vt_spec = pl.BlockSpec((None, nsc, width, KEY_CHUNK), lambda bi, h, i: (bi, 0, h, 0))
    return pl.pallas_call(
        _sb_prompt_kernel,
        grid=(b, SB_HEADS // SB_HEADS_PER_STEP, t // QBLK),
        in_specs=[blk, full, vt_spec],
        out_specs=blk,
        out_shape=jax.ShapeDtypeStruct(q.shape, BF16),
        scratch_shapes=[pltpu.VMEM((SB_HEADS_PER_STEP, 1, QBLK), F32),
                        pltpu.VMEM((SB_HEADS_PER_STEP, SB_HEAD_DIM, QBLK), F32)],
        compiler_params=_cparams(("parallel", "parallel", "arbitrary"), VMEM_LIMIT),
        name="sb_attention_prompt",
    )(q, k, vt)


SELECT_PAGES = 16
ATTN_PAGES = 8
QPAD = SUBLANES


def _page_specs(n_slots, rows, width, layer, order):
    def spec(r):
        return pl.BlockSpec((None, None, rows, width),
                            lambda b, j, pt: (layer, pt[b, order(j, r)], 0, 0))
    return [spec(r) for r in range(n_slots)]


def _head_rows(page_ref, head, n_heads):
    return page_ref[pl.ds(head, PAGE, stride=n_heads), :]


def _dsa_sample_select_kernel(pt_ref, qi_ref, wi_ref, *refs, n_pages, n_new, topk):
    ki_refs = refs[:SELECT_PAGES]
    kin_ref, sel_ref, score_ref, wib_ref, qih_ref = refs[SELECT_PAGES:]
    j = pl.program_id(1)
    shape = (QPAD, LANES)

    @pl.when(j == 0)
    def _():
        qi = qi_ref[...].astype(F32)
        qih_ref[...] = jnp.concatenate(
            [qi[:, h * IDX_DIM:(h + 1) * IDX_DIM] for h in range(IDX_HEADS)], axis=0).astype(BF16)
        wi = wi_ref[...]
        for h in range(IDX_HEADS):
            wib_ref[h] = jnp.broadcast_to(wi[:, h:h + 1], shape)

    def scores(ki_t):
        lg = _dot(qih_ref[...], ki_t.astype(BF16))
        return _indexer_scores(lg, wib_ref, QPAD)

    for r, su in enumerate(scores(jnp.concatenate([r[...] for r in ki_refs], axis=1))):
        score_ref[j * SELECT_PAGES + r] = su

    @pl.when(j == pl.num_programs(1) - 1)
    def _():
        row = lax.broadcasted_iota(I32, shape, 0)
        col = lax.broadcasted_iota(I32, shape, 1)
        visible = (col <= row) & (row < n_new)
        score_ref[n_pages] = jnp.where(visible, scores(kin_ref[...])[0], -jnp.inf)
        nchunk = n_pages + 1

        def count(x, strict):
            accs = [jnp.zeros(shape, F32) for _ in range(4)]
            for c in range(nchunk):
                s = score_ref[c]
                accs[c % 4] = accs[c % 4] + jnp.where((s > x) if strict else (s >= x), 1.0, 0.0)
            return jnp.sum((accs[0] + accs[1]) + (accs[2] + accs[3]), axis=-1, keepdims=True)

        t = _kth_largest(lambda x: count(x, False), float(topk), shape)
        need = jnp.broadcast_to(float(topk) - count(t, True), shape)

        tiles = [score_ref[c] for c in range(nchunk)]
        eqf = [jnp.where(s == t, 1.0, 0.0) for s in tiles]
        eqf += [jnp.zeros(shape, F32)] * (-nchunk % 2)
        cnt = _dot(jnp.concatenate(eqf, axis=0).astype(BF16), _count_matrix())
        carry = jnp.zeros(shape, F32)
        for c in range(nchunk):
            cc = cnt[c * QPAD:(c + 1) * QPAD]
            sel = (tiles[c] > t) | ((tiles[c] == t) & (cc[:, :QBLK] + carry < need))
            if c == n_pages:
                sel = sel & visible
            sel_ref[c] = jnp.where(sel, 1.0, 0.0)
            carry = carry + cc[:, QBLK:]


def dsa_sample_select(page_table, qi8, wi8, cache_idx_k_t, layer, ki_new_t, n_new, topk):
    nb, n_pages = page_table.shape
    assert n_pages % SELECT_PAGES == 0
    per_b = lambda shape: pl.BlockSpec((None,) + shape, lambda b, j, pt: (b,) + (0,) * len(shape))
    grid_spec = pltpu.PrefetchScalarGridSpec(
        num_scalar_prefetch=1,
        grid=(nb, n_pages // SELECT_PAGES),
        in_specs=[per_b((QPAD, IDX_HEADS * IDX_DIM)), per_b((QPAD, IDX_HEADS))]
        + _page_specs(SELECT_PAGES, IDX_DIM, PAGE, layer, lambda j, r: j * SELECT_PAGES + r)
        + [per_b((IDX_DIM, PAGE))],
        out_specs=per_b((n_pages + 1, QPAD, LANES)),
        scratch_shapes=[pltpu.VMEM((n_pages + 1, QPAD, LANES), F32),
                        pltpu.VMEM((IDX_HEADS, QPAD, LANES), F32),
                        pltpu.VMEM((IDX_HEADS * QPAD, IDX_DIM), BF16)],
    )
    return pl.pallas_call(
        functools.partial(_dsa_sample_select_kernel, n_pages=n_pages, n_new=n_new, topk=topk),
        grid_spec=grid_spec,
        out_shape=jax.ShapeDtypeStruct((nb, n_pages + 1, QPAD, LANES), F32),
        compiler_params=_cparams(("parallel", "arbitrary"), VMEM_LIMIT),
        name="dsa_sample_select",
    )(page_table, qi8, wi8, *([cache_idx_k_t] * SELECT_PAGES), ki_new_t)


def _dsa_sample_attn_kernel(pt_ref, q_ref, sel_ref, *refs, n_pages):
    k_refs = refs[:ATTN_PAGES]
    v_refs = refs[ATTN_PAGES:2 * ATTN_PAGES]
    kn_ref, vn_ref, o_ref, qn_ref, m_ref, l_ref, acc_ref = refs[2 * ATTN_PAGES:]
    j = pl.program_id(1)

    @pl.when(j == 0)
    def _():
        q = q_ref[...].astype(F32)
        for n in range(DSA_KV_HEADS):
            qn_ref[n] = jnp.concatenate(
                [q[:, (n * DSA_GROUP + g) * HEAD_DIM:(n * DSA_GROUP + g + 1) * HEAD_DIM]
                 for g in range(DSA_GROUP)], axis=0).astype(BF16)
        m_ref[...] = jnp.full(m_ref.shape, NEG_BIG, F32)
        l_ref[...] = jnp.zeros(l_ref.shape, F32)
        acc_ref[...] = jnp.zeros(acc_ref.shape, F32)

    def attend(k_of, v_of, sels):
        groups = []
        for n in range(DSA_KV_HEADS):
            heads = [n * DSA_GROUP + g for g in range(DSA_GROUP)]
            groups.append((_dot_nt(qn_ref[n], k_of(n).astype(BF16)), v_of(n).astype(BF16), heads))
        _softmax_update(groups, sels, m_ref, l_ref, acc_ref, QPAD)

    gather = lambda page_refs: (lambda n: jnp.concatenate(
        [_head_rows(r, n, DSA_KV_HEADS) for r in page_refs], axis=0))
    attend(gather(k_refs), gather(v_refs), [sel_ref[j * ATTN_PAGES + r] > 0.5 for r in range(ATTN_PAGES)])

    @pl.when(j == pl.num_programs(1) - 1)
    def _():
        head = lambda ref: (lambda n: ref[:, n * HEAD_DIM:(n + 1) * HEAD_DIM])
        attend(head(kn_ref), head(vn_ref), [sel_ref[n_pages] > 0.5])
        for h in range(DSA_HEADS):
            o_ref[:, h * HEAD_DIM:(h + 1) * HEAD_DIM] = acc_ref[h] / l_ref[h]


def dsa_sample_attention(page_table, q8, sel, cache_k, cache_v, layer, k_new_pad, v_new_pad):
    nb, n_pages = page_table.shape
    assert n_pages % ATTN_PAGES == 0
    width = DSA_KV_HEADS * HEAD_DIM
    per_b = lambda shape: pl.BlockSpec((None,) + shape, lambda b, j, pt: (b,) + (0,) * len(shape))
    order = lambda j, r: j * ATTN_PAGES + r
    pages = lambda: _page_specs(ATTN_PAGES, PAGE * DSA_KV_HEADS, HEAD_DIM, layer, order)
    hshape = (DSA_HEADS, QPAD, LANES)
    grid_spec = pltpu.PrefetchScalarGridSpec(
        num_scalar_prefetch=1,
        grid=(nb, n_pages // ATTN_PAGES),
        in_specs=[per_b((QPAD, DSA_HEADS * HEAD_DIM)), per_b((n_pages + 1, QPAD, LANES))]
        + pages() + pages() + [per_b((PAGE, width)), per_b((PAGE, width))],
        out_specs=per_b((QPAD, DSA_HEADS * HEAD_DIM)),
        scratch_shapes=[pltpu.VMEM((DSA_KV_HEADS, DSA_GROUP * QPAD, HEAD_DIM), BF16),
                        pltpu.VMEM(hshape, F32), pltpu.VMEM(hshape, F32), pltpu.VMEM(hshape, F32)],
    )
    return pl.pallas_call(
        functools.partial(_dsa_sample_attn_kernel, n_pages=n_pages),
        grid_spec=grid_spec,
        out_shape=jax.ShapeDtypeStruct((nb, QPAD, DSA_HEADS * HEAD_DIM), F32),
        compiler_params=_cparams(("parallel", "arbitrary"), VMEM_LIMIT),
        name="dsa_sample_attention",
    )(page_table, q8, sel, *([cache_k] * ATTN_PAGES), *([cache_v] * ATTN_PAGES), k_new_pad, v_new_pad)


def _sb_sample_kernel(pt_ref, q_ref, *refs, n_new):
    k_refs = refs[:ATTN_PAGES]
    v_refs = refs[ATTN_PAGES:2 * ATTN_PAGES]
    kn_ref, vn_ref, o_ref, qh_ref, carry_ref, acc_ref = refs[2 * ATTN_PAGES:]
    j = pl.program_id(1)
    suffix_mat = _suffix_matrix()
    rows = SB_HEADS * QPAD

    def attend(k_of, v_of, strict):
        qh = qh_ref[...]
        z = jnp.concatenate([_dot_nt(qh, k_of(h).astype(BF16))[h * QPAD:(h + 1) * QPAD]
                             for h in range(SB_HEADS)], axis=0)
        carry, w = _sb_weights(z, carry_ref[...], suffix_mat, strict)
        wb = w.astype(BF16)
        pv = jnp.concatenate([_dot(wb, v_of(h).astype(BF16))[h * QPAD:(h + 1) * QPAD]
                              for h in range(SB_HEADS)], axis=0)
        carry_ref[...] = carry
        acc_ref[...] = acc_ref[...] + pv

    @pl.when(j == 0)
    def _():
        q = q_ref[...].astype(F32)
        qh_ref[...] = jnp.concatenate(
            [q[:, h * SB_HEAD_DIM:(h + 1) * SB_HEAD_DIM] for h in range(SB_HEADS)], axis=0).astype(BF16)
        carry_ref[...] = jnp.zeros(carry_ref.shape, F32)
        acc_ref[...] = jnp.zeros(acc_ref.shape, F32)
        row = lax.broadcasted_iota(I32, (rows, LANES), 0) % QPAD
        col = lax.broadcasted_iota(I32, (rows, LANES), 1)
        strict = (col < row) & (row < n_new)
        head = lambda ref: (lambda h: ref[:, h * SB_HEAD_DIM:(h + 1) * SB_HEAD_DIM])
        attend(head(kn_ref), head(vn_ref), strict)

    gather = lambda page_refs: (lambda h: jnp.concatenate(
        [_head_rows(r, h, SB_HEADS) for r in reversed(page_refs)], axis=0))
    attend(gather(k_refs), gather(v_refs), None)

    @pl.when(j == pl.num_programs(1) - 1)
    def _():
        for h in range(SB_HEADS):
            o_ref[:, h * SB_HEAD_DIM:(h + 1) * SB_HEAD_DIM] = acc_ref[h * QPAD:(h + 1) * QPAD, :]


def sb_sample_attention(page_table, q8, cache_k, cache_v, layer, k_new_pad, v_new_pad, n_new):
    nb, n_pages = page_table.shape
    assert n_pages % ATTN_PAGES == 0
    width = SB_HEADS * SB_HEAD_DIM
    per_b = lambda shape: pl.BlockSpec((None,) + shape, lambda b, j, pt: (b,) + (0,) * len(shape))
    order = lambda j, r: n_pages - 1 - (j * ATTN_PAGES + r)
    pages = lambda: _page_specs(ATTN_PAGES, PAGE * SB_HEADS, SB_HEAD_DIM, layer, order)
    grid_spec = pltpu.PrefetchScalarGridSpec(
        num_scalar_prefetch=1,
        grid=(nb, n_pages // ATTN_PAGES),
        in_specs=[per_b((QPAD, width))] + pages() + pages() + [per_b((PAGE, width)), per_b((PAGE, width))],
        out_specs=per_b((QPAD, width)),
        scratch_shapes=[pltpu.VMEM((SB_HEADS * QPAD, SB_HEAD_DIM), BF16),
                        pltpu.VMEM((SB_HEADS * QPAD, LANES), F32),
                        pltpu.VMEM((SB_HEADS * QPAD, SB_HEAD_DIM), F32)],
    )
    return pl.pallas_call(
        functools.partial(_sb_sample_kernel, n_new=n_new),
        grid_spec=grid_spec,
        out_shape=jax.ShapeDtypeStruct((nb, QPAD, width), F32),
        compiler_params=_cparams(("parallel", "arbitrary"), VMEM_LIMIT),
        name="sb_sample_attention",
    )(page_table, q8, *([cache_k] * ATTN_PAGES), *([cache_v] * ATTN_PAGES), k_new_pad, v_new_pad)


def _pad_rows(a, nb, n_new, rows):
    a = a.reshape(nb, n_new, a.shape[-1])
    return jnp.pad(a, ((0, 0), (0, rows - n_new), (0, 0)))


def _token_head_rows(cache):
    return cache.reshape(cache.shape[:2] + (cache.shape[2] * cache.shape[3], cache.shape[4]))


def dsa_sample(page_table, q, qi, wi, k_new, v_new, ki_new, cache_k, cache_v, cache_idx_k, layer, topk):
    nb = page_table.shape[0]
    n_new = q.shape[0] // nb
    sel = dsa_sample_select(page_table, _pad_rows(qi, nb, n_new, QPAD), _pad_rows(wi, nb, n_new, QPAD),
                            jnp.swapaxes(cache_idx_k, 2, 3), layer,
                            jnp.swapaxes(_pad_rows(ki_new, nb, n_new, PAGE), 1, 2), n_new, topk)
    o = dsa_sample_attention(page_table, _pad_rows(q, nb, n_new, QPAD), sel,
                             _token_head_rows(cache_k), _token_head_rows(cache_v), layer,
                             _pad_rows(k_new, nb, n_new, PAGE), _pad_rows(v_new, nb, n_new, PAGE))
    return o[:, :n_new].reshape(nb * n_new, -1)


def sb_sample(page_table, q, k_new, v_new, cache_k, cache_v, layer):
    nb = page_table.shape[0]
    n_new = q.shape[0] // nb
    o = sb_sample_attention(page_table, _pad_rows(q, nb, n_new, QPAD),
                            _token_head_rows(cache_k), _token_head_rows(cache_v), layer,
                            _pad_rows(k_new, nb, n_new, PAGE), _pad_rows(v_new, nb, n_new, PAGE), n_new)
    return o[:, :n_new].reshape(nb * n_new, -1)


S5_TILE_U = 128
S5_TILE_S = S5_TILE_U // S5_GROUP * S5_STATE
S5_NTILE = D_MODEL // S5_TILE_U
S5_TILES_PER_HALF = S5_NTILE // 2
S5_LANE_TILES = S5_HALF // LANES


def _s5_kernel(x_ref, g_ref, d_ref, bx_ref, ct_ref, are_ref, aim_ref, h0re_ref, h0im_ref,
               y_ref, hre_out, him_out, xre, xim, hre, him, *, nb, tc, t_last, split_input):
    c = pl.program_id(0)
    rows_half = (nb // 2) * tc

    @pl.when(c == 0)
    def _():
        hre[...] = h0re_ref[...]
        him[...] = h0im_ref[...]

    u = _rms(x_ref[...].reshape(rows_half, D_MODEL), g_ref[...])
    u_hi, u_lo = _split_bf16(u)
    lanes_per_tile = S5_TILE_S // LANES
    for tile in range(S5_NTILE):
        half, ctile = divmod(tile, S5_TILES_PER_HALF)
        rows = slice(half * rows_half, (half + 1) * rows_half)
        ucol = slice(tile * S5_TILE_U, (tile + 1) * S5_TILE_U)
        a_hi, a_lo = u_hi[:, ucol], u_lo[:, ucol]
        re = _dot(a_hi, bx_ref[0, tile])
        im = _dot(a_hi, bx_ref[2, tile])
        if split_input:
            re = re + _dot(a_lo, bx_ref[0, tile]) + _dot(a_hi, bx_ref[1, tile])
            im = im + _dot(a_lo, bx_ref[2, tile]) + _dot(a_hi, bx_ref[3, tile])
        for k in range(lanes_per_tile):
            lt = ctile * lanes_per_tile + k
            xre[lt, rows, :] = re[:, k * LANES:(k + 1) * LANES]
            xim[lt, rows, :] = im[:, k * LANES:(k + 1) * LANES]

    def step(t, carry):
        idx = pl.ds(t, nb, stride=tc)
        new = []
        for lt in range(S5_LANE_TILES):
            hr, hi = carry[lt]
            ar = are_ref[lt]
            ai = aim_ref[lt]
            nr = ar * hr - ai * hi + xre[lt, idx, :]
            ni = ar * hi + ai * hr + xim[lt, idx, :]
            xre[lt, idx, :] = nr
            xim[lt, idx, :] = ni
            new.append((nr, ni))
        return tuple(new)

    h_end = lax.fori_loop(0, tc, step, tuple((hre[lt], him[lt]) for lt in range(S5_LANE_TILES)))
    for lt in range(S5_LANE_TILES):
        hre[lt] = h_end[lt][0]
        him[lt] = h_end[lt][1]

    ys = []
    for tile in range(S5_NTILE):
        half, ctile = divmod(tile, S5_TILES_PER_HALF)
        rows = slice(half * rows_half, (half + 1) * rows_half)
        lts = range(ctile * lanes_per_tile, (ctile + 1) * lanes_per_tile)
        h_re = jnp.concatenate([xre[lt, rows, :] for lt in lts], axis=1).astype(BF16)
        h_im = jnp.concatenate([xim[lt, rows, :] for lt in lts], axis=1).astype(BF16)
        ys.append(_dot(h_re, ct_ref[0, tile]) + _dot(h_im, ct_ref[1, tile]))
    y = jnp.concatenate(ys, axis=1) + d_ref[...] * u
    y_ref[...] = y.reshape(y_ref.shape)

    @pl.when(c == t_last // tc)
    def _():
        idx = pl.ds(t_last % tc, nb, stride=tc)
        for lt in range(S5_LANE_TILES):
            hre_out[lt] = xre[lt, idx, :]
            him_out[lt] = xim[lt, idx, :]


def _s5_discretize(log_dt, a_re, a_im, b_re, b_im, c_re, c_im):
    dt = jnp.exp(log_dt)[:, None]
    decay = jnp.exp(dt * a_re)
    ang = dt * a_im
    ab_re = decay * jnp.cos(ang)
    ab_im = decay * jnp.sin(ang)
    den = a_re * a_re + a_im * a_im
    nr = ab_re - 1.0
    cf_re = (nr * a_re + ab_im * a_im) / den
    cf_im = (ab_im * a_re - nr * a_im) / den
    bx_re = cf_re[:, :, None] * b_re - cf_im[:, :, None] * b_im
    bx_im = cf_re[:, :, None] * b_im + cf_im[:, :, None] * b_re
    gpt = S5_TILE_U // S5_GROUP
    eye = jnp.eye(gpt, dtype=F32)

    def b_tiles(b):
        bt = b.reshape(S5_NTILE, gpt, S5_STATE, S5_GROUP)
        return jnp.einsum('tgpc,gh->tgchp', bt, eye).reshape(S5_NTILE, S5_TILE_U, S5_TILE_S)

    def c_tiles(cm):
        ctl = cm.reshape(S5_NTILE, gpt, S5_GROUP, S5_STATE)
        return jnp.einsum('tgcp,gh->tgphc', ctl, eye).reshape(S5_NTILE, S5_TILE_S, S5_TILE_U)

    bre_hi, bre_lo = _split_bf16(b_tiles(bx_re))
    bim_hi, bim_lo = _split_bf16(b_tiles(bx_im))
    bx = jnp.stack([bre_hi, bre_lo, bim_hi, bim_lo])
    ct = jnp.stack([c_tiles(c_re), -c_tiles(c_im)]).astype(BF16)
    return ab_re.reshape(-1), ab_im.reshape(-1), bx, ct


def _s5_rows(vec, n_seq):
    v = vec.reshape(2, 1, S5_LANE_TILES, LANES)
    return jnp.broadcast_to(v, (2, n_seq, S5_LANE_TILES, LANES)).transpose(2, 0, 1, 3).reshape(
        S5_LANE_TILES, 2 * n_seq, LANES)


def _s5_state_to_rows(h):
    n = h.shape[0]
    return h.reshape(n, 2, S5_LANE_TILES, LANES).transpose(2, 1, 0, 3).reshape(S5_LANE_TILES, 2 * n, LANES)


def _s5_rows_to_state(r):
    n = r.shape[1] // 2
    return r.reshape(S5_LANE_TILES, 2, n, LANES).transpose(2, 1, 0, 3).reshape(n, S5_GROUPS, S5_STATE)


def _s5_chunk_len(t_total):
    best = None
    for c in range(SUBLANES, QBLK + 1, 2 * SUBLANES):
        if t_total % c == 0:
            best = c
    assert best is not None
    return best


def s5_scan(x3, g, d_skip, disc, h0_re, h0_im, n_seq, tc, t_last, split_input):
    ab_re, ab_im, bx, ct = disc
    gdim, t_total, _ = x3.shape
    rchunk = n_seq * tc // gdim
    assert t_total % rchunk == 0
    nb = 2 * n_seq
    nchunks = t_total // rchunk
    st = (S5_LANE_TILES, nb, LANES)
    buf = (S5_LANE_TILES, nb * tc, LANES)
    y, hre, him = pl.pallas_call(
        functools.partial(_s5_kernel, nb=nb, tc=tc, t_last=t_last, split_input=split_input),
        grid=(nchunks,),
        in_specs=[pl.BlockSpec((gdim, rchunk, D_MODEL), lambda c: (0, c, 0)),
                  _const_spec((1, D_MODEL)), _const_spec((1, D_MODEL)),
                  _const_spec(bx.shape), _const_spec(ct.shape),
                  _const_spec(st), _const_spec(st), _const_spec(st), _const_spec(st)],
        out_specs=[pl.BlockSpec((gdim, rchunk, D_MODEL), lambda c: (0, c, 0)),
                   pl.BlockSpec(st, lambda c: (0, 0, 0)), pl.BlockSpec(st, lambda c: (0, 0, 0))],
        out_shape=[jax.ShapeDtypeStruct(x3.shape, F32),
                   jax.ShapeDtypeStruct(st, F32), jax.ShapeDtypeStruct(st, F32)],
        scratch_shapes=[pltpu.VMEM(buf, F32), pltpu.VMEM(buf, F32),
                        pltpu.VMEM(st, F32), pltpu.VMEM(st, F32)],
        compiler_params=_cparams(("arbitrary",), VMEM_LIMIT),
        name="s5_scan",
    )(x3, g.reshape(1, D_MODEL), d_skip.reshape(1, D_MODEL), bx, ct,
      _s5_rows(ab_re, n_seq), _s5_rows(ab_im, n_seq), _s5_state_to_rows(h0_re), _s5_state_to_rows(h0_im))
    return y, _s5_rows_to_state(hre), _s5_rows_to_state(him)


def kernel(x_prompt, x_sample, cache_dsa_k, cache_dsa_v, cache_dsa_idx_k, state_ssm_re, state_ssm_im,
           cache_sb_k, cache_sb_v, page_table, meta_tokens, g_mix, g_ffn,
           dsa_w_in, dsa_q_norm, dsa_k_norm, dsa_w_out,
           s5_log_dt, s5_a_re, s5_a_im, s5_b_re, s5_b_im, s5_c_re, s5_c_im, s5_d, s5_w_out,
           sb_w_in, sb_w_out, ffn_w_gu, ffn_w_down, moe_w_router, moe_b_router, moe_w_gu, moe_w_down):
    bsz, seq, _ = x_prompt.shape
    t_p = seq + N_META
    t_pad = -(-t_p // KEY_CHUNK) * KEY_CHUNK
    meta = jnp.broadcast_to(meta_tokens[None], (bsz, N_META, D_MODEL))
    xp = jnp.concatenate([meta, x_prompt, jnp.zeros((bsz, t_pad - t_p, D_MODEL), F32)], axis=1)
    xp = xp.reshape(bsz * t_pad, D_MODEL)
    nb, n_new, _ = x_sample.shape
    xs = x_sample.reshape(nb * n_new, D_MODEL)
    past_len = page_table.shape[1] * PAGE
    topk_p = min(TOPK_MAX, seq // 4)
    topk_s = min(TOPK_MAX, (past_len + n_new) // 4)
    tm_p = PROMPT_TILE
    tm_s = nb * n_new

    p3 = lambda a: a.reshape(bsz, t_pad, a.shape[-1])
    out_p = lambda a, *hd: a.reshape((bsz, t_p) + hd)
    out_s = lambda a, *hd: a.reshape((nb, n_new) + hd)

    outs = {name: [] for name in ("dsa_k_p", "dsa_k_s", "dsa_v_p", "dsa_v_s", "dsa_i_p", "dsa_i_s",
                                  "ssm_re_p", "ssm_re_s", "ssm_im_p", "ssm_im_s",
                                  "sb_k_p", "sb_k_s", "sb_v_p", "sb_v_s")}
    ia = ib = ic = i_dense = i_moe = 0
    depth = g_mix.shape[0]
    for layer in range(depth):
        kind = layer % 3
        if kind == 0:
            w_in = jnp.concatenate(
                [dsa_w_in[ia], jnp.zeros((D_MODEL, DSA_IN_PAD - dsa_w_in.shape[2]), F32)], axis=1).astype(BF16)
            w_out = dsa_w_out[ia].astype(BF16)
            q, k, v, qi, ki, wi, kb, vb, ki_all = dsa_project(
                p3(xp), g_mix[layer], w_in, dsa_q_norm[ia], dsa_k_norm[ia], KEY_CHUNK, t_p)
            o = dsa_attention_prompt(q, qi, wi, kb, vb, ki_all, topk_p)
            xp = out_project(xp, o.reshape(bsz * t_pad, D_MODEL), w_out, tm_p)
            qs, ks, vs, qis, kis, wis = [a[0] for a in dsa_project(
                xs[None], g_mix[layer], w_in, dsa_q_norm[ia], dsa_k_norm[ia], tm_s, tm_s)[:6]]
            o_s = dsa_sample(page_table, qs, qis, wis, ks, vs, kis,
                             cache_dsa_k, cache_dsa_v, cache_dsa_idx_k, ia, topk_s)
            xs = out_project(xs, o_s.astype(BF16), w_out, tm_s)
            outs["dsa_k_p"].append(out_p(k, DSA_KV_HEADS, HEAD_DIM))
            outs["dsa_v_p"].append(out_p(v, DSA_KV_HEADS, HEAD_DIM))
            outs["dsa_i_p"].append(out_p(ki, IDX_DIM))
            outs["dsa_k_s"].append(out_s(ks, DSA_KV_HEADS, HEAD_DIM))
            outs["dsa_v_s"].append(out_s(vs, DSA_KV_HEADS, HEAD_DIM))
            outs["dsa_i_s"].append(out_s(kis, IDX_DIM))
            ia += 1
        elif kind == 1:
            disc = _s5_discretize(s5_log_dt[ib], s5_a_re[ib], s5_a_im[ib], s5_b_re[ib], s5_b_im[ib],
                                  s5_c_re[ib], s5_c_im[ib])
            w_out = s5_w_out[ib].astype(BF16)
            zeros = jnp.zeros((bsz, S5_GROUPS, S5_STATE), F32)
            yp, hre_p, him_p = s5_scan(p3(xp), g_mix[layer], s5_d[ib], disc, zeros, zeros, bsz,
                                       _s5_chunk_len(t_pad), t_p - 1, False)
            xp = s5_glu(xp, yp.reshape(bsz * t_pad, D_MODEL), w_out, tm_p)
            ys, hre_s, him_s = s5_scan(xs.reshape(1, nb * n_new, D_MODEL), g_mix[layer], s5_d[ib], disc,
                                       state_ssm_re[ib], state_ssm_im[ib], nb, n_new, n_new - 1, True)
            xs = s5_glu(xs, ys.reshape(nb * n_new, D_MODEL), w_out, tm_s)
            outs["ssm_re_p"].append(hre_p)
            outs["ssm_im_p"].append(him_p)
            outs["ssm_re_s"].append(hre_s)
            outs["ssm_im_s"].append(him_s)
            ib += 1
        else:
            w_in = sb_w_in[ic].astype(BF16)
            w_out = sb_w_out[ic].astype(BF16)
            q, k, v, kb, vb = sb_project(p3(xp), g_mix[layer], w_in, KEY_CHUNK, t_p)
            o = sb_attention_prompt(q, kb, vb)
            xp = out_project(xp, o.reshape(bsz * t_pad, D_MODEL), w_out, tm_p)
            qs, ks, vs = [a[0] for a in sb_project(xs[None], g_mix[layer], w_in, tm_s, tm_s)[:3]]
            o_s = sb_sample(page_table, qs, ks, vs, cache_sb_k, cache_sb_v, ic)
            xs = out_project(xs, o_s.astype(BF16), w_out, tm_s)
            outs["sb_k_p"].append(out_p(k, SB_HEADS, SB_HEAD_DIM))
            outs["sb_v_p"].append(out_p(v, SB_HEADS, SB_HEAD_DIM))
            outs["sb_k_s"].append(out_s(ks, SB_HEADS, SB_HEAD_DIM))
            outs["sb_v_s"].append(out_s(vs, SB_HEADS, SB_HEAD_DIM))
            ic += 1
        if layer % 2 == 0:
            wgu = ffn_w_gu[i_dense].astype(BF16)
            wd = ffn_w_down[i_dense].astype(BF16)
            xp = ffn(xp, g_ffn[layer], wgu, wd, tm_p)
            xs = ffn(xs, g_ffn[layer], wgu, wd, tm_s)
            i_dense += 1
        else:
            wr = jnp.concatenate([moe_w_router[i_moe], jnp.zeros((D_MODEL, LANES - N_EXPERTS), F32)], axis=1)
            br = jnp.concatenate([moe_b_router[i_moe], jnp.full((LANES - N_EXPERTS,), NEG_BIG, F32)])
            br = br.reshape(1, LANES)
            wgu = moe_w_gu[i_moe].astype(BF16)
            wd = moe_w_down[i_moe].astype(BF16)
            xp = moe_routed(xp, g_ffn[layer], wr, br, wgu, wd, MOE_TILE)
            xs = moe(xs, g_ffn[layer], wr, br, wgu, wd, tm_s)
            i_moe += 1

    y_prompt = xp.reshape(bsz, t_pad, D_MODEL)[:, N_META:t_p]
    y_sample = xs.reshape(nb, n_new, D_MODEL)
    stack = lambda name: jnp.stack(outs[name])
    return (y_prompt, y_sample,
            stack("dsa_k_p"), stack("dsa_k_s"), stack("dsa_v_p"), stack("dsa_v_s"),
            stack("dsa_i_p"), stack("dsa_i_s"),
            stack("ssm_re_p"), stack("ssm_re_s"), stack("ssm_im_p"), stack("ssm_im_s"),
            stack("sb_k_p"), stack("sb_k_s"), stack("sb_v_p"), stack("sb_v_s"))
```
